```python
import jax, jax.numpy as jnp
from jax import lax
import numpy as np

D_MODEL = 1024
BATCH = 8
SEQ = 4096
DEPTH = 4

CHUNK = 64
N_MIXERS = 2
N_A_LAYERS = (DEPTH + 1) // 2
N_B_LAYERS = DEPTH // 2
N_VRES = max(N_B_LAYERS - 1, 0)
PLE_DIM = 256
D_FF = 2816
RMS_EPS = 1e-6
GMLP_BLOCK = 128
GMLP_WIDTH = 2 * D_MODEL
GMLP_GROUPS = 8
GMLP_GROUP_DIM = GMLP_WIDTH // GMLP_GROUPS
LN_EPS = 1e-5
RWKV_HEAD_DIM = 64
RWKV_HEADS = D_MODEL // RWKV_HEAD_DIM
DECAY_LORA = 64
AAA_LORA = 64
MV_LORA = 32
GATE_LORA = 160
GN_EPS = 64e-5

kernel_name = "hybrid_gmlp_rwkv7_macaron_trunk"


def rms_norm(x, g):
    xf = x.astype(jnp.float32)
    y = xf * lax.rsqrt(jnp.mean(xf * xf, axis=-1, keepdims=True) + RMS_EPS)
    return (y * g.astype(jnp.float32)).astype(x.dtype)


def swiglu(x, w13, w2):
    gate, up = jnp.split(x @ w13, 2, axis=-1)
    return (jax.nn.silu(gate) * up) @ w2


def gmlp_mixer(h, w_in, b_in, v_gain, w_s, b_s, w_out):
    B, S, _ = h.shape
    z = jax.nn.gelu(h @ w_in + b_in)
    u, v = jnp.split(z, 2, axis=-1)
    vf = v.astype(jnp.float32)
    mean = jnp.mean(vf, axis=-1, keepdims=True)
    var = jnp.mean(jnp.square(vf - mean), axis=-1, keepdims=True)
    v = ((vf - mean) * lax.rsqrt(var + LN_EPS) * v_gain.astype(jnp.float32)).astype(h.dtype)
    nb = S // GMLP_BLOCK
    v = v.reshape(B, nb, GMLP_BLOCK, GMLP_GROUPS, GMLP_GROUP_DIM)
    u = u.reshape(B, nb, GMLP_BLOCK, GMLP_GROUPS, GMLP_GROUP_DIM)
    cidx = jnp.arange(GMLP_BLOCK) // CHUNK
    mask = cidx[None, :] <= cidx[:, None]
    ws = jnp.where(mask[None], w_s, jnp.zeros_like(w_s))
    s = jnp.einsum('gij,bcjgd->bcigd', ws, v) + b_s.T[None, None, :, :, None]
    y = (u * s).reshape(B, S, GMLP_WIDTH)
    return y @ w_out


def wkv7_scan(r, w, k, v, a, b):
    B, S, H, N = r.shape

    def step(state, inp):
        r_t, w_t, k_t, v_t, a_t, b_t = inp
        sa = jnp.einsum('bhij,bhj->bhi', state, a_t)
        state = (state * w_t[:, :, None, :] + sa[..., None] * b_t[:, :, None, :]
                 + v_t[..., None] * k_t[:, :, None, :])
        y_t = jnp.einsum('bhij,bhj->bhi', state, r_t)
        return state, y_t

    xs = tuple(jnp.moveaxis(t, 1, 0) for t in (r, w, k, v, a, b))
    state0 = jnp.zeros((B, H, N, N), jnp.float32)
    _, ys = lax.scan(step, state0, xs)
    return jnp.moveaxis(ys, 0, 1)


def rwkv7_mixer(h, v_first, vres, mu, w_in, w0, w1, w2, a0, a1, a2, g1, g2,
                k_k, k_a, r_k, lnx, w_out):
    B, S, D = h.shape
    H, N = RWKV_HEADS, RWKV_HEAD_DIM
    f32 = jnp.float32
    dx = jnp.pad(h, ((0, 0), (1, 0), (0, 0)))[:, :-1] - h
    x_rkv = h[:, :, None, :] + dx[:, :, None, :] * mu[:3]
    rkv = jnp.einsum('bsnd,nde->bsne', x_rkv, w_in)
    r, k, v = rkv[:, :, 0], rkv[:, :, 1], rkv[:, :, 2]
    xw = h + dx * mu[3]
    xa = h + dx * mu[4]
    xg = h + dx * mu[5]
    zw = (w0 + jnp.tanh(xw @ w1) @ w2).astype(f32)
    decay = jnp.exp(-jnp.exp(-jax.nn.softplus(-zw) - 0.5))
    a = jax.nn.sigmoid(a0 + (xa @ a1) @ a2)
    g = jax.nn.sigmoid(xg @ g1) @ g2

    def heads(t):
        return t.reshape(B, S, H, N).astype(f32)

    kk = heads(k * k_k)
    kk = kk / jnp.maximum(jnp.sqrt(jnp.sum(kk * kk, axis=-1, keepdims=True)), 1e-12)
    k = k * (1.0 + (a - 1.0) * k_a)
    if vres is None:
        v_first = v
    else:
        v0, v1, v2 = vres
        xv = h + dx * mu[2]
        v = v + (v_first - v) * jax.nn.sigmoid(v0 + (xv @ v1) @ v2)
    rh, kh, vh, ah = heads(r), heads(k), heads(v), heads(a)
    y = wkv7_scan(rh, heads(decay), kh, vh, -kk, kk * ah)
    mean = jnp.mean(y, axis=-1, keepdims=True)
    var = jnp.mean(jnp.square(y - mean), axis=-1, keepdims=True)
    yn = ((y - mean) * lax.rsqrt(var + GN_EPS)).reshape(B, S, D)
    yn = yn * lnx[0].astype(f32) + lnx[1].astype(f32)
    bonus = (jnp.sum(rh * kh * r_k.astype(f32), axis=-1, keepdims=True) * vh).reshape(B, S, D)
    out = ((yn + bonus).astype(h.dtype) * g) @ w_out
    return out, v_first


def setup_inputs(seed: int = 0) -> dict:
    key = jax.random.key(seed)
    ks = iter(jax.random.split(key, 40))
    f32 = jnp.float32

    def nrm(shape, scale):
        return jax.random.normal(next(ks), shape, f32) * scale

    def gain(shape):
        return 1.0 + nrm(shape, 0.05)

    D, E2, E = D_MODEL, 2 * GMLP_WIDTH, GMLP_WIDTH
    NA, NB = N_A_LAYERS, N_B_LAYERS
    return {
        "x": nrm((BATCH, SEQ, D), 1.0),
        "p": nrm((DEPTH, BATCH, SEQ, PLE_DIM), 1.0),
        "norm_g": gain((DEPTH, 8, D)),
        "ffn_w13": nrm((DEPTH, 2, D, 2 * D_FF), D ** -0.5),
        "ffn_w2": nrm((DEPTH, 2, D_FF, D), D_FF ** -0.5),
        "ple_w_gate": nrm((DEPTH, D, D), D ** -0.5),
        "ple_w_proj": nrm((DEPTH, PLE_DIM, D), PLE_DIM ** -0.5),
        "a_w_in": nrm((NA, D, E2), D ** -0.5),
        "a_b_in": nrm((NA, E2), 0.02),
        "a_v_gain": gain((NA, E)),
        "a_w_s": nrm((NA, GMLP_GROUPS, GMLP_BLOCK, GMLP_BLOCK), GMLP_BLOCK ** -0.5),
        "a_b_s": gain((NA, GMLP_GROUPS, GMLP_BLOCK)),
        "a_w_out": nrm((NA, E, D), E ** -0.5),
        "b_mu": jax.random.uniform(next(ks), (NB, 6, D), f32),
        "b_w_in": nrm((NB, 3, D, D), D ** -0.5),
        "b_w0": nrm((NB, D), 0.5),
        "b_w1": nrm((NB, D, DECAY_LORA), D ** -0.5),
        "b_w2": nrm((NB, DECAY_LORA, D), DECAY_LORA ** -0.5),
        "b_a0": nrm((NB, D), 0.1),
        "b_a1": nrm((NB, D, AAA_LORA), D ** -0.5),
        "b_a2": nrm((NB, AAA_LORA, D), AAA_LORA ** -0.5),
        "b_g1": nrm((NB, D, GATE_LORA), D ** -0.5),
        "b_g2": nrm((NB, GATE_LORA, D), GATE_LORA ** -0.5),
        "b_k_k": 0.85 + nrm((NB, D), 0.05),
        "b_k_a": gain((NB, D)),
        "b_r_k": nrm((NB, RWKV_HEADS, RWKV_HEAD_DIM), 0.1),
        "b_lnx": jnp.stack([gain((NB, D)), nrm((NB, D), 0.02)], axis=1),
        "b_w_out": nrm((NB, D, D), D ** -0.5),
        "b_v0": nrm((N_VRES, D), 0.1),
        "b_v1": nrm((N_VRES, D, MV_LORA), D ** -0.5),
        "b_v2": nrm((N_VRES, MV_LORA, D), MV_LORA ** -0.5),
    }


def reference(x, p, norm_g, ffn_w13, ffn_w2, ple_w_gate, ple_w_proj,
              a_w_in, a_b_in, a_v_gain, a_w_s, a_b_s, a_w_out,
              b_mu, b_w_in, b_w0, b_w1, b_w2, b_a0, b_a1, b_a2, b_g1, b_g2,
              b_k_k, b_k_a, b_r_k, b_lnx, b_w_out, b_v0, b_v1, b_v2):
    h = x
    v_first = None
    for i in range(DEPTH):
        g = norm_g[i]
        h = h + 0.5 * rms_norm(swiglu(rms_norm(h, g[0]), ffn_w13[i, 0], ffn_w2[i, 0]), g[1])
        hn = rms_norm(h, g[2])
        j = i // N_MIXERS
        if i % N_MIXERS == 0:
            m = gmlp_mixer(hn, a_w_in[j], a_b_in[j], a_v_gain[j], a_w_s[j], a_b_s[j], a_w_out[j])
        else:
            vres = None if v_first is None else (b_v0[j - 1], b_v1[j - 1], b_v2[j - 1])
            m, v_first = rwkv7_mixer(hn, v_first, vres, b_mu[j], b_w_in[j], b_w0[j], b_w1[j],
                                     b_w2[j], b_a0[j], b_a1[j], b_a2[j], b_g1[j], b_g2[j],
                                     b_k_k[j], b_k_a[j], b_r_k[j], b_lnx[j], b_w_out[j])
        h = h + rms_norm(m, g[3])
        h = h + 0.5 * rms_norm(swiglu(rms_norm(h, g[4]), ffn_w13[i, 1], ffn_w2[i, 1]), g[5])
        gate = jax.nn.sigmoid(rms_norm(h, g[6]) @ ple_w_gate[i])
        h = h + rms_norm(gate * (p[i] @ ple_w_proj[i]), g[7])
    return h
```

```python
import functools
import math

import jax
import jax.numpy as jnp
from jax import lax
from jax.experimental import pallas as pl
from jax.experimental.pallas import tpu as pltpu

F32 = jnp.float32
BF16 = jnp.bfloat16

RMS_EPS = 1e-6
LN_EPS = 1e-5
GN_EPS = 64e-5
KK_EPS_SQ = 1e-24
DECAY_SCALE = math.exp(-0.5)

HEAD_DIM = 64
GMLP_BLOCK = 128
GMLP_GROUPS = 8
CHUNK = 64
LANES = 128

V7X_VMEM_LIMIT_BYTES = 56 * 1024 * 1024


def _dot(a, b):
    return jnp.dot(a, b, preferred_element_type=F32)


def _dot_nt(a, b):
    return lax.dot_general(a, b, (((1,), (1,)), ((), ())), preferred_element_type=F32)


def _dot_tn(a, b):
    return lax.dot_general(a, b, (((0,), (0,)), ((), ())), preferred_element_type=F32)


def _rms(x, g):
    return x * lax.rsqrt(jnp.mean(x * x, axis=-1, keepdims=True) + RMS_EPS) * g


def _resident(shape):
    nd = len(shape)
    return pl.BlockSpec(shape, lambda *_: (0,) * nd, pipeline_mode=pl.Buffered(1))


def _params(n_axes=1):
    return pltpu.CompilerParams(
        dimension_semantics=("arbitrary",) * n_axes,
        vmem_limit_bytes=V7X_VMEM_LIMIT_BYTES)


def _ffn_kernel(*refs, d_ff, f_chunk, g_in, g_out, with_ple):
    if with_ple:
        h_ref, g_ref, w13_ref, w2_ref, p_ref, wg_ref, wp_ref, o_ref, acc_ref = refs
    else:
        h_ref, g_ref, w13_ref, w2_ref, o_ref, acc_ref = refs
    h = h_ref[...]
    xn = _rms(h, g_ref[g_in:g_in + 1, :]).astype(BF16)
    for j in range(d_ff // f_chunk):
        lo = j * f_chunk
        gate = _dot(xn, w13_ref[:, lo:lo + f_chunk])
        up = _dot(xn, w13_ref[:, d_ff + lo:d_ff + lo + f_chunk])
        act = (gate * jax.nn.sigmoid(gate) * up).astype(BF16)
        part = _dot(act, w2_ref[lo:lo + f_chunk, :])
        if j == 0:
            acc_ref[...] = part
        else:
            acc_ref[...] += part
    h = h + 0.5 * _rms(acc_ref[...], g_ref[g_out:g_out + 1, :])
    if with_ple:
        hn = _rms(h, g_ref[6:7, :]).astype(BF16)
        gate = jax.nn.sigmoid(_dot(hn, wg_ref[...]))
        proj = _dot(p_ref[...].astype(BF16), wp_ref[...])
        h = h + _rms(gate * proj, g_ref[7:8, :])
    o_ref[...] = h


def _ffn(h, g, w13, w2, *, g_in, g_out, ple=None, tm=512, f_chunk=256):
    t, d = h.shape
    d_ff = w2.shape[0]
    tm = min(tm, t)
    row = lambda i: (i, 0)
    in_specs = [pl.BlockSpec((tm, d), row), _resident(g.shape),
                _resident(w13.shape), _resident(w2.shape)]
    args = [h, g, w13, w2]
    if ple is not None:
        p, wg, wp = ple
        in_specs += [pl.BlockSpec((tm, p.shape[1]), row), _resident(wg.shape), _resident(wp.shape)]
        args += [p, wg, wp]
    kern = functools.partial(_ffn_kernel, d_ff=d_ff, f_chunk=f_chunk, g_in=g_in, g_out=g_out,
                             with_ple=ple is not None)
    return pl.pallas_call(
        kern,
        grid=(t // tm,),
        in_specs=in_specs,
        out_specs=pl.BlockSpec((tm, d), row),
        out_shape=jax.ShapeDtypeStruct((t, d), F32),
        scratch_shapes=[pltpu.VMEM((tm, d), F32)],
        compiler_params=_params(),
        name="ffn_ple" if ple is not None else "ffn",
    )(*args)


def _gmlp_kernel(h_ref, g_ref, win_ref, bin_ref, vgain_ref, ws_ref, bst_ref, wout_ref,
                 o_ref, v_scr, y_scr, *, width):
    tm = h_ref.shape[0]
    gw = width // GMLP_GROUPS
    h = h_ref[...]
    hn = _rms(h, g_ref[2:3, :]).astype(BF16)
    for g in range(GMLP_GROUPS):
        lo = width + g * gw
        v_scr[:, g * gw:(g + 1) * gw] = jax.nn.gelu(_dot(hn, win_ref[:, lo:lo + gw]) + bin_ref[:, lo:lo + gw])
    v = v_scr[...]
    mean = jnp.mean(v, axis=-1, keepdims=True)
    vc = v - mean
    var = jnp.mean(vc * vc, axis=-1, keepdims=True)
    v_scr[...] = vc * lax.rsqrt(var + LN_EPS) * vgain_ref[...]
    ci = lax.broadcasted_iota(jnp.int32, (GMLP_BLOCK, GMLP_BLOCK), 0) // CHUNK
    cj = lax.broadcasted_iota(jnp.int32, (GMLP_BLOCK, GMLP_BLOCK), 1) // CHUNK
    causal = cj <= ci
    for g in range(GMLP_GROUPS):
        ws = jnp.where(causal, ws_ref[g], 0.0).astype(BF16)
        bias = bst_ref[:, g:g + 1]
        u = jax.nn.gelu(_dot(hn, win_ref[:, g * gw:(g + 1) * gw]) + bin_ref[:, g * gw:(g + 1) * gw])
        for c in range(tm // GMLP_BLOCK):
            rows = slice(c * GMLP_BLOCK, (c + 1) * GMLP_BLOCK)
            s = _dot(ws, v_scr[rows, g * gw:(g + 1) * gw].astype(BF16)) + bias
            y_scr[rows, g * gw:(g + 1) * gw] = (u[rows, :] * s).astype(BF16)
    m = _dot(y_scr[...], wout_ref[...])
    o_ref[...] = h + _rms(m, g_ref[3:4, :])


def _gmlp(h, g, w_in, b_in, v_gain, w_s, b_s_t, w_out, *, tm=256):
    t, d = h.shape
    width = w_out.shape[0]
    tm = min(tm, t)
    row = lambda i: (i, 0)
    return pl.pallas_call(
        functools.partial(_gmlp_kernel, width=width),
        grid=(t // tm,),
        in_specs=[pl.BlockSpec((tm, d), row), _resident(g.shape), _resident(w_in.shape),
                  _resident(b_in.shape), _resident(v_gain.shape), _resident(w_s.shape),
                  _resident(b_s_t.shape), _resident(w_out.shape)],
        out_specs=pl.BlockSpec((tm, d), row),
        out_shape=jax.ShapeDtypeStruct((t, d), F32),
        scratch_shapes=[pltpu.VMEM((tm, width), F32), pltpu.VMEM((tm, width), BF16)],
        compiler_params=_params(),
        name="gmlp",
    )(h, g, w_in, b_in, v_gain, w_s, b_s_t, w_out)


def _rwkv_in_kernel(*refs, tiles_per_seq, with_vres):
    (h_ref, g_ref, mu_ref, wr_ref, wk_ref, wv_ref, w0_ref, w1_ref, w2_ref, a0_ref, a1_ref, a2_ref,
     g1_ref, g2_ref, kk_ref, ka_ref, rk_ref, hsum_ref) = refs[:18]
    if with_vres:
        vf_ref, v0_ref, v1_ref, v2_ref = refs[18:22]
        outs = refs[22:]
    else:
        outs = refs[18:]
    r_o, lw_o, k_o, v_o, an_o, bn_o, gate_o, bonus_o, carry = outs
    tm = h_ref.shape[0]

    @pl.when(pl.program_id(0) % tiles_per_seq == 0)
    def _():
        carry[...] = jnp.zeros_like(carry)

    hn = _rms(h_ref[...], g_ref[2:3, :])
    first = lax.broadcasted_iota(jnp.int32, (tm, 1), 0) == 0
    prev = jnp.where(first, carry[0:1, :], pltpu.roll(hn, 1, axis=0))
    carry[0:1, :] = hn[tm - 1:tm, :]
    dx = prev - hn

    def lerp(i):
        return (hn + dx * mu_ref[i:i + 1, :]).astype(BF16)

    x_v = lerp(2)
    r = _dot(lerp(0), wr_ref[...])
    k = _dot(lerp(1), wk_ref[...])
    v = _dot(x_v, wv_ref[...])
    zw = w0_ref[...] + _dot(jnp.tanh(_dot(lerp(3), w1_ref[...])).astype(BF16), w2_ref[...])
    lw_o[...] = -DECAY_SCALE * jax.nn.sigmoid(zw)
    a = jax.nn.sigmoid(a0_ref[...] + _dot(_dot(lerp(4), a1_ref[...]).astype(BF16), a2_ref[...]))
    gate_o[...] = _dot(jax.nn.sigmoid(_dot(lerp(5), g1_ref[...])).astype(BF16), g2_ref[...])

    kk = k * kk_ref[...]
    ss = _dot((kk * kk).astype(BF16), hsum_ref[...])
    kk = kk * lax.rsqrt(jnp.maximum(ss, KK_EPS_SQ))
    k = k * (1.0 + (a - 1.0) * ka_ref[...])
    if with_vres:
        mix = jax.nn.sigmoid(v0_ref[...] + _dot(_dot(x_v, v1_ref[...]).astype(BF16), v2_ref[...]))
        v = v + (vf_ref[...] - v) * mix
    r_o[...] = r
    k_o[...] = k
    v_o[...] = v
    an_o[...] = -kk
    bn_o[...] = kk * a
    bonus_o[...] = _dot((r * k * rk_ref[...]).astype(BF16), hsum_ref[...]) * v


def _rwkv_in(h, g, mu, wr, wk, wv, w0, w1, w2, a0, a1, a2, g1, g2, k_k, k_a, r_k, hsum,
             vres, *, seq, tm=256):
    t, d = h.shape
    tm = min(tm, seq)
    row = lambda i: (i, 0)
    tile = pl.BlockSpec((tm, d), row)
    args = [h, g, mu, wr, wk, wv, w0, w1, w2, a0, a1, a2, g1, g2, k_k, k_a, r_k, hsum]
    in_specs = [tile] + [_resident(x.shape) for x in args[1:]]
    if vres is not None:
        v_first, v0, v1, v2 = vres
        args += [v_first, v0, v1, v2]
        in_specs += [tile, _resident(v0.shape), _resident(v1.shape), _resident(v2.shape)]
    kern = functools.partial(_rwkv_in_kernel, tiles_per_seq=seq // tm, with_vres=vres is not None)
    return pl.pallas_call(
        kern,
        grid=(t // tm,),
        in_specs=in_specs,
        out_specs=[tile] * 8,
        out_shape=[jax.ShapeDtypeStruct((t, d), F32)] * 8,
        scratch_shapes=[pltpu.VMEM((8, d), F32)],
        compiler_params=_params(),
        name="rwkv_in",
    )(*args)


def _wkv_kernel(r_ref, lw_ref, k_ref, v_ref, a_ref, b_ref, y_ref, s_scr, *, steps_per_seq):
    L = CHUNK
    d = r_ref.shape[1]

    @pl.when(pl.program_id(0) % steps_per_seq == 0)
    def _():
        s_scr[...] = jnp.zeros_like(s_scr)

    lw = lw_ref[...]
    ri = lax.broadcasted_iota(jnp.int32, (L, L), 0)
    cj = lax.broadcasted_iota(jnp.int32, (L, L), 1)
    tri = jnp.where(cj <= ri, 1.0, 0.0).astype(BF16)
    hi = lw.astype(BF16)
    rest = lw - hi.astype(F32)
    mid = rest.astype(BF16)
    low = (rest - mid.astype(F32)).astype(BF16)
    cum = _dot(tri, hi) + _dot(tri, mid) + _dot(tri, low)
    cum_last = cum[L - 1:L, :]
    p_in = jnp.exp(cum)
    p_inv = jnp.exp(-cum)
    p_ex = jnp.exp(cum - lw)
    p_dec = jnp.exp(cum_last - cum)
    p_last = jnp.exp(cum_last)

    r = r_ref[...]
    k = k_ref[...]
    a = a_ref[...]
    b = b_ref[...]
    rt = (r * p_in).astype(BF16)
    at = (a * p_ex).astype(BF16)
    kt = (k * p_inv).astype(BF16)
    bt = (b * p_inv).astype(BF16)
    kd = (k * p_dec).astype(BF16)
    bd = (b * p_dec).astype(BF16)
    vb = v_ref[...].astype(BF16)

    head0 = lax.broadcasted_iota(jnp.int32, (1, LANES), 1) < HEAD_DIM
    ti = lax.broadcasted_iota(jnp.int32, (L, LANES), 0)
    sj = lax.broadcasted_iota(jnp.int32, (L, LANES), 1) % HEAD_DIM
    strict = sj < ti
    incl = sj <= ti
    eye = jnp.where(sj == ti, 1.0, 0.0)

    def stack(x):
        zero = jnp.zeros_like(x)
        return jnp.concatenate([jnp.where(head0, x, zero), jnp.where(head0, zero, x)], axis=0)

    for p in range(d // LANES):
        cols = slice(p * LANES, (p + 1) * LANES)
        at_p, rt_p, v_p = at[:, cols], rt[:, cols], vb[:, cols]
        g = _dot_nt(jnp.concatenate([at_p, rt_p], axis=0),
                    jnp.concatenate([stack(bt[:, cols]), stack(kt[:, cols])], axis=0))
        a_ab = jnp.where(strict, g[:L, :LANES], 0.0)
        a_ak = jnp.where(strict, g[:L, LANES:], 0.0).astype(BF16)
        a_rb = jnp.where(incl, g[L:, :LANES], 0.0).astype(BF16)
        a_rk = jnp.where(incl, g[L:, LANES:], 0.0).astype(BF16)
        minv = eye + a_ab
        apow = a_ab.astype(BF16)
        for i in range(5):
            apow_f = _dot(apow, stack(apow))
            apow = apow_f.astype(BF16)
            minv = minv + _dot(apow, stack(minv.astype(BF16)))
        minv = minv.astype(BF16)

        s = s_scr[p]
        sb = s.astype(BF16)
        v2 = stack(v_p)
        at_m = _dot(minv, stack(at_p)).astype(BF16)
        w = _dot(minv, stack(_dot(a_ak, v2).astype(BF16)))
        u = _dot_nt(at_m, sb) + w
        u2 = stack(u.astype(BF16))
        y_ref[:, cols] = _dot_nt(rt_p, sb) + _dot(a_rb, u2) + _dot(a_rk, v2)
        upd = _dot_tn(jnp.concatenate([u2, v2], axis=0),
                      jnp.concatenate([stack(bd[:, cols]), stack(kd[:, cols])], axis=0))
        s_scr[p] = s * p_last[:, cols] + upd


def _wkv(r, lw, k, v, an, bn, *, seq):
    t, d = r.shape
    row = lambda i: (i, 0)
    tile = pl.BlockSpec((CHUNK, d), row)
    return pl.pallas_call(
        functools.partial(_wkv_kernel, steps_per_seq=seq // CHUNK),
        grid=(t // CHUNK,),
        in_specs=[tile] * 6,
        out_specs=tile,
        out_shape=jax.ShapeDtypeStruct((t, d), F32),
        scratch_shapes=[pltpu.VMEM((d // LANES, LANES, LANES), F32)],
        compiler_params=_params(),
        name="wkv",
    )(r, lw, k, v, an, bn)


def _rwkv_out_kernel(h_ref, y_ref, bonus_ref, gate_ref, g_ref, lnx_ref, hmean_ref, wout_ref, o_ref):
    y = y_ref[...]
    mean = _dot(y.astype(BF16), hmean_ref[...])
    yc = y - mean
    var = _dot((yc * yc).astype(BF16), hmean_ref[...])
    yn = yc * lax.rsqrt(var + GN_EPS) * lnx_ref[0:1, :] + lnx_ref[1:2, :]
    out = _dot(((yn + bonus_ref[...]) * gate_ref[...]).astype(BF16), wout_ref[...])
    o_ref[...] = h_ref[...] + _rms(out, g_ref[3:4, :])


def _rwkv_out(h, y, bonus, gate, g, lnx, hmean, w_out, *, tm=512):
    t, d = h.shape
    tm = min(tm, t)
    tile = pl.BlockSpec((tm, d), lambda i: (i, 0))
    return pl.pallas_call(
        _rwkv_out_kernel,
        grid=(t // tm,),
        in_specs=[tile] * 4 + [_resident(g.shape), _resident(lnx.shape), _resident(hmean.shape),
                               _resident(w_out.shape)],
        out_specs=tile,
        out_shape=jax.ShapeDtypeStruct((t, d), F32),
        compiler_params=_params(),
        name="rwkv_out",
    )(h, y, bonus, gate, g, lnx, hmean, w_out)


def kernel(x, p, norm_g, ffn_w13, ffn_w2, ple_w_gate, ple_w_proj, a_w_in, a_b_in, a_v_gain, a_w_s, a_b_s, a_w_out, b_mu, b_w_in, b_w0, b_w1, b_w2, b_a0, b_a1, b_a2, b_g1, b_g2, b_k_k, b_k_a, b_r_k, b_lnx, b_w_out, b_v0, b_v1, b_v2):
    bsz, seq, d = x.shape
    depth = norm_g.shape[0]
    t = bsz * seq
    bf = lambda w: w.astype(BF16)
    row = lambda w: w.reshape(1, -1)

    head = jnp.arange(d) // HEAD_DIM
    same_head = head[:, None] == head[None, :]
    hsum = same_head.astype(BF16)
    hmean = (same_head.astype(F32) / HEAD_DIM).astype(BF16)

    h = x.reshape(t, d)
    v_first = None
    for i in range(depth):
        g = norm_g[i]
        j = i // 2
        h = _ffn(h, g, bf(ffn_w13[i, 0]), bf(ffn_w2[i, 0]), g_in=0, g_out=1)
        if i % 2 == 0:
            h = _gmlp(h, g, bf(a_w_in[j]), row(a_b_in[j]), row(a_v_gain[j]), a_w_s[j],
                      a_b_s[j].T, bf(a_w_out[j]))
        else:
            vres = None
            if v_first is not None:
                vres = (v_first, row(b_v0[j - 1]), bf(b_v1[j - 1]), bf(b_v2[j - 1]))
            r, lw, k, v, an, bn, gate, bonus = _rwkv_in(
                h, g, b_mu[j], bf(b_w_in[j, 0]), bf(b_w_in[j, 1]), bf(b_w_in[j, 2]),
                row(b_w0[j]), bf(b_w1[j]), bf(b_w2[j]), row(b_a0[j]), bf(b_a1[j]), bf(b_a2[j]),
                bf(b_g1[j]), bf(b_g2[j]), row(b_k_k[j]), row(b_k_a[j]), row(b_r_k[j]), hsum,
                vres, seq=seq)
            if v_first is None:
                v_first = v
            y = _wkv(r, lw, k, v, an, bn, seq=seq)
            h = _rwkv_out(h, y, bonus, gate, g, b_lnx[j], hmean, bf(b_w_out[j]))
        h = _ffn(h, g, bf(ffn_w13[i, 1]), bf(ffn_w2[i, 1]), g_in=4, g_out=5,
                 ple=(p[i].reshape(t, -1), bf(ple_w_gate[i]), bf(ple_w_proj[i])))
    return h.reshape(bsz, seq, d)
```

```python
import functools
import math

import jax
import jax.numpy as jnp
from jax import lax
from jax.experimental import pallas as pl
from jax.experimental.pallas import tpu as pltpu

F32 = jnp.float32
BF16 = jnp.bfloat16

RMS_EPS = 1e-6
LN_EPS = 1e-5
GN_EPS = 64e-5
KK_EPS_SQ = 1e-24
DECAY_SCALE = math.exp(-0.5)

HEAD_DIM = 64
GMLP_BLOCK = 128
GMLP_GROUPS = 8
CHUNK = 64
LANES = 128

V7X_VMEM_LIMIT_BYTES = 56 * 1024 * 1024


def _dot(a, b):
    return jnp.dot(a, b, preferred_element_type=F32)


def _dot_nt(a, b):
    return lax.dot_general(a, b, (((1,), (1,)), ((), ())), preferred_element_type=F32)


def _dot_tn(a, b):
    return lax.dot_general(a, b, (((0,), (0,)), ((), ())), preferred_element_type=F32)


def _rms(x, g):
    return x * lax.rsqrt(jnp.mean(x * x, axis=-1, keepdims=True) + RMS_EPS) * g


def _resident(shape):
    nd = len(shape)
    return pl.BlockSpec(shape, lambda *_: (0,) * nd, pipeline_mode=pl.Buffered(1))


def _params(n_axes=1):
    return pltpu.CompilerParams(
        dimension_semantics=("arbitrary",) * n_axes,
        vmem_limit_bytes=V7X_VMEM_LIMIT_BYTES)


def _ffn_kernel(*refs, d_ff, f_chunk, g_in, g_out, with_ple):
    if with_ple:
        h_ref, g_ref, w13_ref, w2_ref, p_ref, wg_ref, wp_ref, o_ref, acc_ref = refs
    else:
        h_ref, g_ref, w13_ref, w2_ref, o_ref, acc_ref = refs
    h = h_ref[...]
    xn = _rms(h, g_ref[g_in:g_in + 1, :]).astype(BF16)
    for j in range(d_ff // f_chunk):
        lo = j * f_chunk
        gate = _dot(xn, w13_ref[:, lo:lo + f_chunk])
        up = _dot(xn, w13_ref[:, d_ff + lo:d_ff + lo + f_chunk])
        act = (gate * jax.nn.sigmoid(gate) * up).astype(BF16)
        part = _dot(act, w2_ref[lo:lo + f_chunk, :])
        if j == 0:
            acc_ref[...] = part
        else:
            acc_ref[...] += part
    h = h + 0.5 * _rms(acc_ref[...], g_ref[g_out:g_out + 1, :])
    if with_ple:
        hn = _rms(h, g_ref[6:7, :]).astype(BF16)
        gate = jax.nn.sigmoid(_dot(hn, wg_ref[...]))
        proj = _dot(p_ref[...].astype(BF16), wp_ref[...])
        h = h + _rms(gate * proj, g_ref[7:8, :])
    o_ref[...] = h


def _ffn(h, g, w13, w2, *, g_in, g_out, ple=None, tm=512, f_chunk=256):
    t, d = h.shape
    d_ff = w2.shape[0]
    tm = min(tm, t)
    row = lambda i: (i, 0)
    in_specs = [pl.BlockSpec((tm, d), row), _resident(g.shape),
                _resident(w13.shape), _resident(w2.shape)]
    args = [h, g, w13, w2]
    if ple is not None:
        p, wg, wp = ple
        in_specs += [pl.BlockSpec((tm, p.shape[1]), row), _resident(wg.shape), _resident(wp.shape)]
        args += [p, wg, wp]
    kern = functools.partial(_ffn_kernel, d_ff=d_ff, f_chunk=f_chunk, g_in=g_in, g_out=g_out,
                             with_ple=ple is not None)
    return pl.pallas_call(
        kern,
        grid=(t // tm,),
        in_specs=in_specs,
        out_specs=pl.BlockSpec((tm, d), row),
        out_shape=jax.ShapeDtypeStruct((t, d), F32),
        scratch_shapes=[pltpu.VMEM((tm, d), F32)],
        compiler_params=_params(),
        name="ffn_ple" if ple is not None else "ffn",
    )(*args)


def _gmlp_kernel(h_ref, g_ref, win_ref, bin_ref, vgain_ref, ws_ref, bst_ref, wout_ref,
                 o_ref, v_scr, y_scr, *, width):
    tm = h_ref.shape[0]
    gw = width // GMLP_GROUPS
    h = h_ref[...]
    hn = _rms(h, g_ref[2:3, :]).astype(BF16)
    for g in range(GMLP_GROUPS):
        lo = width + g * gw
        v_scr[:, g * gw:(g + 1) * gw] = jax.nn.gelu(_dot(hn, win_ref[:, lo:lo + gw]) + bin_ref[:, lo:lo + gw])
    v = v_scr[...]
    mean = jnp.mean(v, axis=-1, keepdims=True)
    vc = v - mean
    var = jnp.mean(vc * vc, axis=-1, keepdims=True)
    v_scr[...] = vc * lax.rsqrt(var + LN_EPS) * vgain_ref[...]
    ci = lax.broadcasted_iota(jnp.int32, (GMLP_BLOCK, GMLP_BLOCK), 0) // CHUNK
    cj = lax.broadcasted_iota(jnp.int32, (GMLP_BLOCK, GMLP_BLOCK), 1) // CHUNK
    causal = cj <= ci
    for g in range(GMLP_GROUPS):
        ws = jnp.where(causal, ws_ref[g], 0.0).astype(BF16)
        bias = bst_ref[:, g:g + 1]
        u = jax.nn.gelu(_dot(hn, win_ref[:, g * gw:(g + 1) * gw]) + bin_ref[:, g * gw:(g + 1) * gw])
        for c in range(tm // GMLP_BLOCK):
            rows = slice(c * GMLP_BLOCK, (c + 1) * GMLP_BLOCK)
            s = _dot(ws, v_scr[rows, g * gw:(g + 1) * gw].astype(BF16)) + bias
            y_scr[rows, g * gw:(g + 1) * gw] = (u[rows, :] * s).astype(BF16)
    m = _dot(y_scr[...], wout_ref[...])
    o_ref[...] = h + _rms(m, g_ref[3:4, :])


def _gmlp(h, g, w_in, b_in, v_gain, w_s, b_s_t, w_out, *, tm=256):
    t, d = h.shape
    width = w_out.shape[0]
    tm = min(tm, t)
    row = lambda i: (i, 0)
    return pl.pallas_call(
        functools.partial(_gmlp_kernel, width=width),
        grid=(t // tm,),
        in_specs=[pl.BlockSpec((tm, d), row), _resident(g.shape), _resident(w_in.shape),
                  _resident(b_in.shape), _resident(v_gain.shape), _resident(w_s.shape),
                  _resident(b_s_t.shape), _resident(w_out.shape)],
        out_specs=pl.BlockSpec((tm, d), row),
        out_shape=jax.ShapeDtypeStruct((t, d), F32),
        scratch_shapes=[pltpu.VMEM((tm, width), F32), pltpu.VMEM((tm, width), BF16)],
        compiler_params=_params(),
        name="gmlp",
    )(h, g, w_in, b_in, v_gain, w_s, b_s_t, w_out)


def _rwkv_in_kernel(*refs, tiles_per_seq, with_vres):
    (h_ref, g_ref, mu_ref, wr_ref, wk_ref, wv_ref, w0_ref, w1_ref, w2_ref, a0_ref, a1_ref, a2_ref,
     g1_ref, g2_ref, kk_ref, ka_ref, rk_ref, hsum_ref) = refs[:18]
    if with_vres:
        vf_ref, v0_ref, v1_ref, v2_ref = refs[18:22]
        outs = refs[22:]
    else:
        outs = refs[18:]
    r_o, lw_o, k_o, v_o, an_o, bn_o, gate_o, bonus_o, carry = outs
    tm = h_ref.shape[0]

    @pl.when(pl.program_id(0) % tiles_per_seq == 0)
    def _():
        carry[...] = jnp.zeros_like(carry)

    hn = _rms(h_ref[...], g_ref[2:3, :])
    first = lax.broadcasted_iota(jnp.int32, (tm, 1), 0) == 0
    prev = jnp.where(first, carry[0:1, :], pltpu.roll(hn, 1, axis=0))
    carry[0:1, :] = hn[tm - 1:tm, :]
    dx = prev - hn

    def lerp(i):
        return (hn + dx * mu_ref[i:i + 1, :]).astype(BF16)

    x_v = lerp(2)
    r = _dot(lerp(0), wr_ref[...])
    k = _dot(lerp(1), wk_ref[...])
    v = _dot(x_v, wv_ref[...])
    zw = w0_ref[...] + _dot(jnp.tanh(_dot(lerp(3), w1_ref[...])).astype(BF16), w2_ref[...])
    lw_o[...] = -DECAY_SCALE * jax.nn.sigmoid(zw)
    a = jax.nn.sigmoid(a0_ref[...] + _dot(_dot(lerp(4), a1_ref[...]).astype(BF16), a2_ref[...]))
    gate_o[...] = _dot(jax.nn.sigmoid(_dot(lerp(5), g1_ref[...])).astype(BF16), g2_ref[...])

    kk = k * kk_ref[...]
    ss = _dot((kk * kk).astype(BF16), hsum_ref[...])
    kk = kk * lax.rsqrt(jnp.maximum(ss, KK_EPS_SQ))
    k = k * (1.0 + (a - 1.0) * ka_ref[...])
    if with_vres:
        mix = jax.nn.sigmoid(v0_ref[...] + _dot(_dot(x_v, v1_ref[...]).astype(BF16), v2_ref[...]))
        v = v + (vf_ref[...] - v) * mix
    r_o[...] = r
    k_o[...] = k
    v_o[...] = v
    an_o[...] = -kk
    bn_o[...] = kk * a
    bonus_o[...] = _dot((r * k * rk_ref[...]).astype(BF16), hsum_ref[...]) * v


def _rwkv_in(h, g, mu, wr, wk, wv, w0, w1, w2, a0, a1, a2, g1, g2, k_k, k_a, r_k, hsum,
             vres, *, seq, tm=256):
    t, d = h.shape
    tm = min(tm, seq)
    row = lambda i: (i, 0)
    tile = pl.BlockSpec((tm, d), row)
    args = [h, g, mu, wr, wk, wv, w0, w1, w2, a0, a1, a2, g1, g2, k_k, k_a, r_k, hsum]
    in_specs = [tile] + [_resident(x.shape) for x in args[1:]]
    if vres is not None:
        v_first, v0, v1, v2 = vres
        args += [v_first, v0, v1, v2]
        in_specs += [tile, _resident(v0.shape), _resident(v1.shape), _resident(v2.shape)]
    kern = functools.partial(_rwkv_in_kernel, tiles_per_seq=seq // tm, with_vres=vres is not None)
    return pl.pallas_call(
        kern,
        grid=(t // tm,),
        in_specs=in_specs,
        out_specs=[tile] * 8,
        out_shape=[jax.ShapeDtypeStruct((t, d), F32)] * 8,
        scratch_shapes=[pltpu.VMEM((8, d), F32)],
        compiler_params=_params(),
        name="rwkv_in",
    )(*args)


def _wkv_kernel(r_ref, lw_ref, k_ref, v_ref, a_ref, b_ref, y_ref, s_scr, *, steps_per_seq):
    L = CHUNK
    d = r_ref.shape[1]

    @pl.when(pl.program_id(0) % steps_per_seq == 0)
    def _():
        s_scr[...] = jnp.zeros_like(s_scr)

    lw = lw_ref[...]
    ri = lax.broadcasted_iota(jnp.int32, (L, L), 0)
    cj = lax.broadcasted_iota(jnp.int32, (L, L), 1)
    tri = jnp.where(cj <= ri, 1.0, 0.0).astype(BF16)
    hi = lw.astype(BF16)
    rest = lw - hi.astype(F32)
    mid = rest.astype(BF16)
    low = (rest - mid.astype(F32)).astype(BF16)
    cum = _dot(tri, hi) + _dot(tri, mid) + _dot(tri, low)
    cum_last = cum[L - 1:L, :]
    p_in = jnp.exp(cum)
    p_inv = jnp.exp(-cum)
    p_ex = jnp.exp(cum - lw)
    p_dec = jnp.exp(cum_last - cum)
    p_last = jnp.exp(cum_last)

    r = r_ref[...]
    k = k_ref[...]
    a = a_ref[...]
    b = b_ref[...]
    rt = (r * p_in).astype(BF16)
    at = (a * p_ex).astype(BF16)
    kt = (k * p_inv).astype(BF16)
    bt = (b * p_inv).astype(BF16)
    kd = (k * p_dec).astype(BF16)
    bd = (b * p_dec).astype(BF16)
    vb = v_ref[...].astype(BF16)

    head0 = lax.broadcasted_iota(jnp.int32, (1, LANES), 1) < HEAD_DIM
    ti = lax.broadcasted_iota(jnp.int32, (L, LANES), 0)
    sj = lax.broadcasted_iota(jnp.int32, (L, LANES), 1) % HEAD_DIM
    strict = sj < ti
    incl = sj <= ti
    eye = jnp.where(sj == ti, 1.0, 0.0)

    def stack(x):
        zero = jnp.zeros_like(x)
        return jnp.concatenate([jnp.where(head0, x, zero), jnp.where(head0, zero, x)], axis=0)

    pairs = range(d // LANES)
    cols = [slice(p * LANES, (p + 1) * LANES) for p in pairs]
    g = [_dot_nt(jnp.concatenate([at[:, c], rt[:, c]], axis=0),
                 jnp.concatenate([stack(bt[:, c]), stack(kt[:, c])], axis=0)) for c in cols]
    a_ab = [jnp.where(strict, x[:L, :LANES], 0.0) for x in g]
    a_ak = [jnp.where(strict, x[:L, LANES:], 0.0).astype(BF16) for x in g]
    a_rb = [jnp.where(incl, x[L:, :LANES], 0.0).astype(BF16) for x in g]
    a_rk = [jnp.where(incl, x[L:, LANES:], 0.0).astype(BF16) for x in g]
    v2 = [stack(vb[:, c]) for c in cols]
    av = [_dot(a_ak[p], v2[p]).astype(BF16) for p in pairs]
    y0 = [_dot(a_rk[p], v2[p]) for p in pairs]
    minv = [eye + x for x in a_ab]
    apow = [x.astype(BF16) for x in a_ab]
    apow = [_dot(x, stack(x)).astype(BF16) for x in apow]
    for i in range(5):
        minv = [minv[p] + _dot(apow[p], stack(minv[p].astype(BF16))) for p in pairs]
        if i < 4:
            apow = [_dot(x, stack(x)).astype(BF16) for x in apow]
    minv = [x.astype(BF16) for x in minv]
    at_m = [_dot(minv[p], stack(at[:, cols[p]])).astype(BF16) for p in pairs]
    w = [_dot(minv[p], stack(av[p])) for p in pairs]

    s = [s_scr[p] for p in pairs]
    sb = [x.astype(BF16) for x in s]
    u2 = [stack((_dot_nt(at_m[p], sb[p]) + w[p]).astype(BF16)) for p in pairs]
    for p in pairs:
        y_ref[:, cols[p]] = _dot_nt(rt[:, cols[p]], sb[p]) + _dot(a_rb[p], u2[p]) + y0[p]
    for p in pairs:
        upd = _dot_tn(jnp.concatenate([u2[p], v2[p]], axis=0),
                      jnp.concatenate([stack(bd[:, cols[p]]), stack(kd[:, cols[p]])], axis=0))
        s_scr[p] = s[p] * p_last[:, cols[p]] + upd


def _wkv(r, lw, k, v, an, bn, *, seq):
    t, d = r.shape
    row = lambda i: (i, 0)
    tile = pl.BlockSpec((CHUNK, d), row)
    return pl.pallas_call(
        functools.partial(_wkv_kernel, steps_per_seq=seq // CHUNK),
        grid=(t // CHUNK,),
        in_specs=[tile] * 6,
        out_specs=tile,
        out_shape=jax.ShapeDtypeStruct((t, d), F32),
        scratch_shapes=[pltpu.VMEM((d // LANES, LANES, LANES), F32)],
        compiler_params=_params(),
        name="wkv",
    )(r, lw, k, v, an, bn)


def _rwkv_out_kernel(h_ref, y_ref, bonus_ref, gate_ref, g_ref, lnx_ref, hmean_ref, wout_ref, o_ref):
    y = y_ref[...]
    mean = _dot(y.astype(BF16), hmean_ref[...])
    yc = y - mean
    var = _dot((yc * yc).astype(BF16), hmean_ref[...])
    yn = yc * lax.rsqrt(var + GN_EPS) * lnx_ref[0:1, :] + lnx_ref[1:2, :]
    out = _dot(((yn + bonus_ref[...]) * gate_ref[...]).astype(BF16), wout_ref[...])
    o_ref[...] = h_ref[...] + _rms(out, g_ref[3:4, :])


def _rwkv_out(h, y, bonus, gate, g, lnx, hmean, w_out, *, tm=512):
    t, d = h.shape
    tm = min(tm, t)
    tile = pl.BlockSpec((tm, d), lambda i: (i, 0))
    return pl.pallas_call(
        _rwkv_out_kernel,
        grid=(t // tm,),
        in_specs=[tile] * 4 + [_resident(g.shape), _resident(lnx.shape), _resident(hmean.shape),
                               _resident(w_out.shape)],
        out_specs=tile,
        out_shape=jax.ShapeDtypeStruct((t, d), F32),
        compiler_params=_params(),
        name="rwkv_out",
    )(h, y, bonus, gate, g, lnx, hmean, w_out)


def kernel(x, p, norm_g, ffn_w13, ffn_w2, ple_w_gate, ple_w_proj, a_w_in, a_b_in, a_v_gain, a_w_s, a_b_s, a_w_out, b_mu, b_w_in, b_w0, b_w1, b_w2, b_a0, b_a1, b_a2, b_g1, b_g2, b_k_k, b_k_a, b_r_k, b_lnx, b_w_out, b_v0, b_v1, b_v2):
    bsz, seq, d = x.shape
    depth = norm_g.shape[0]
    t = bsz * seq
    bf = lambda w: w.astype(BF16)
    row = lambda w: w.reshape(1, -1)

    head = jnp.arange(d) // HEAD_DIM
    same_head = head[:, None] == head[None, :]
    hsum = same_head.astype(BF16)
    hmean = (same_head.astype(F32) / HEAD_DIM).astype(BF16)

    h = x.reshape(t, d)
    v_first = None
    for i in range(depth):
        g = norm_g[i]
        j = i // 2
        h = _ffn(h, g, bf(ffn_w13[i, 0]), bf(ffn_w2[i, 0]), g_in=0, g_out=1)
        if i % 2 == 0:
            h = _gmlp(h, g, bf(a_w_in[j]), row(a_b_in[j]), row(a_v_gain[j]), a_w_s[j],
                      a_b_s[j].T, bf(a_w_out[j]))
        else:
            vres = None
            if v_first is not None:
                vres = (v_first, row(b_v0[j - 1]), bf(b_v1[j - 1]), bf(b_v2[j - 1]))
            r, lw, k, v, an, bn, gate, bonus = _rwkv_in(
                h, g, b_mu[j], bf(b_w_in[j, 0]), bf(b_w_in[j, 1]), bf(b_w_in[j, 2]),
                row(b_w0[j]), bf(b_w1[j]), bf(b_w2[j]), row(b_a0[j]), bf(b_a1[j]), bf(b_a2[j]),
                bf(b_g1[j]), bf(b_g2[j]), row(b_k_k[j]), row(b_k_a[j]), row(b_r_k[j]), hsum,
                vres, seq=seq)
            if v_first is None:
                v_first = v
            y = _wkv(r, lw, k, v, an, bn, seq=seq)
            h = _rwkv_out(h, y, bonus, gate, g, b_lnx[j], hmean, bf(b_w_out[j]))
        h = _ffn(h, g, bf(ffn_w13[i, 1]), bf(ffn_w2[i, 1]), g_in=4, g_out=5,
                 ple=(p[i].reshape(t, -1), bf(ple_w_gate[i]), bf(ple_w_proj[i])))
    return h.reshape(bsz, seq, d)
```

```python
import functools
import math

import jax
import jax.numpy as jnp
from jax import lax
from jax.experimental import pallas as pl
from jax.experimental.pallas import tpu as pltpu

F32 = jnp.float32
BF16 = jnp.bfloat16

RMS_EPS = 1e-6
LN_EPS = 1e-5
GN_EPS = 64e-5
KK_EPS_SQ = 1e-24
DECAY_SCALE = math.exp(-0.5)

HEAD_DIM = 64
GMLP_BLOCK = 128
GMLP_GROUPS = 8
CHUNK = 64
LANES = 128
MXU_WIDTH = 256

V7X_VMEM_LIMIT_BYTES = 56 * 1024 * 1024


def _dot(a, b):
    return jnp.dot(a, b, preferred_element_type=F32)


def _dot_nt(a, b):
    return lax.dot_general(a, b, (((1,), (1,)), ((), ())), preferred_element_type=F32)


def _dot_tn(a, b):
    return lax.dot_general(a, b, (((0,), (0,)), ((), ())), preferred_element_type=F32)


def _rms(x, g):
    return x * lax.rsqrt(jnp.mean(x * x, axis=-1, keepdims=True) + RMS_EPS) * g


def _head_sum(x, hsum):
    w = hsum.shape[0]
    xb = x.astype(BF16)
    return jnp.concatenate([_dot(xb[:, i:i + w], hsum) for i in range(0, x.shape[1], w)], axis=1)


def _gelu_tanh(x):
    c = -2.0 * math.sqrt(2.0 / math.pi) * math.log2(math.e)
    z = x * (x * x * (c * 0.044715) + c)
    return x / (1.0 + jnp.exp2(z))


def _resident(shape):
    nd = len(shape)
    return pl.BlockSpec(shape, lambda *_: (0,) * nd, pipeline_mode=pl.Buffered(1))


def _params(n_axes=1):
    return pltpu.CompilerParams(
        dimension_semantics=("arbitrary",) * n_axes,
        vmem_limit_bytes=V7X_VMEM_LIMIT_BYTES)


def _ffn_kernel(*refs, d_ff, f_chunk, g_in, g_out, with_ple):
    if with_ple:
        h_ref, g_ref, w13_ref, w2_ref, p_ref, wg_ref, wp_ref, o_ref, acc_ref = refs
    else:
        h_ref, g_ref, w13_ref, w2_ref, o_ref, acc_ref = refs
    h = h_ref[...]
    xn = _rms(h, g_ref[g_in:g_in + 1, :]).astype(BF16)
    for j in range(d_ff // f_chunk):
        lo = j * f_chunk
        gate = _dot(xn, w13_ref[:, lo:lo + f_chunk])
        up = _dot(xn, w13_ref[:, d_ff + lo:d_ff + lo + f_chunk])
        act = (gate * jax.nn.sigmoid(gate) * up).astype(BF16)
        part = _dot(act, w2_ref[lo:lo + f_chunk, :])
        if j == 0:
            acc_ref[...] = part
        else:
            acc_ref[...] += part
    h = h + 0.5 * _rms(acc_ref[...], g_ref[g_out:g_out + 1, :])
    if with_ple:
        hn = _rms(h, g_ref[6:7, :]).astype(BF16)
        gate = jax.nn.sigmoid(_dot(hn, wg_ref[...]))
        proj = _dot(p_ref[...].astype(BF16), wp_ref[...])
        h = h + _rms(gate * proj, g_ref[7:8, :])
    o_ref[...] = h


def _ffn(h, g, w13, w2, *, g_in, g_out, ple=None, tm=512, f_chunk=256):
    t, d = h.shape
    d_ff = w2.shape[0]
    tm = min(tm, t)
    row = lambda i: (i, 0)
    in_specs = [pl.BlockSpec((tm, d), row), _resident(g.shape),
                _resident(w13.shape), _resident(w2.shape)]
    args = [h, g, w13, w2]
    if ple is not None:
        p, wg, wp = ple
        in_specs += [pl.BlockSpec((tm, p.shape[1]), row), _resident(wg.shape), _resident(wp.shape)]
        args += [p, wg, wp]
    kern = functools.partial(_ffn_kernel, d_ff=d_ff, f_chunk=f_chunk, g_in=g_in, g_out=g_out,
                             with_ple=ple is not None)
    return pl.pallas_call(
        kern,
        grid=(t // tm,),
        in_specs=in_specs,
        out_specs=pl.BlockSpec((tm, d), row),
        out_shape=jax.ShapeDtypeStruct((t, d), F32),
        scratch_shapes=[pltpu.VMEM((tm, d), F32)],
        compiler_params=_params(),
        name="ffn_ple" if ple is not None else "ffn",
    )(*args)


def _gmlp_kernel(h_ref, g_ref, win_ref, bin_ref, vgain_ref, ws_ref, bst_ref, wout_ref,
                 o_ref, v_scr, y_scr, *, width):
    tm = h_ref.shape[0]
    gw = width // GMLP_GROUPS
    h = h_ref[...]
    hn = _rms(h, g_ref[2:3, :]).astype(BF16)
    for g in range(GMLP_GROUPS):
        lo = width + g * gw
        v_scr[:, g * gw:(g + 1) * gw] = _gelu_tanh(_dot(hn, win_ref[:, lo:lo + gw]) + bin_ref[:, lo:lo + gw])
    v = v_scr[...]
    mean = jnp.mean(v, axis=-1, keepdims=True)
    vc = v - mean
    var = jnp.mean(vc * vc, axis=-1, keepdims=True)
    v_scr[...] = vc * lax.rsqrt(var + LN_EPS) * vgain_ref[...]
    ci = lax.broadcasted_iota(jnp.int32, (GMLP_BLOCK, GMLP_BLOCK), 0) // CHUNK
    cj = lax.broadcasted_iota(jnp.int32, (GMLP_BLOCK, GMLP_BLOCK), 1) // CHUNK
    causal = cj <= ci
    for g in range(GMLP_GROUPS):
        ws = jnp.where(causal, ws_ref[g], 0.0).astype(BF16)
        bias = bst_ref[:, g:g + 1]
        u = _gelu_tanh(_dot(hn, win_ref[:, g * gw:(g + 1) * gw]) + bin_ref[:, g * gw:(g + 1) * gw])
        for c in range(tm // GMLP_BLOCK):
            rows = slice(c * GMLP_BLOCK, (c + 1) * GMLP_BLOCK)
            s = _dot(ws, v_scr[rows, g * gw:(g + 1) * gw].astype(BF16)) + bias
            y_scr[rows, g * gw:(g + 1) * gw] = (u[rows, :] * s).astype(BF16)
    m = _dot(y_scr[...], wout_ref[...])
    o_ref[...] = h + _rms(m, g_ref[3:4, :])


def _gmlp(h, g, w_in, b_in, v_gain, w_s, b_s_t, w_out, *, tm=512):
    t, d = h.shape
    width = w_out.shape[0]
    tm = min(tm, t)
    row = lambda i: (i, 0)
    return pl.pallas_call(
        functools.partial(_gmlp_kernel, width=width),
        grid=(t // tm,),
        in_specs=[pl.BlockSpec((tm, d), row), _resident(g.shape), _resident(w_in.shape),
                  _resident(b_in.shape), _resident(v_gain.shape), _resident(w_s.shape),
                  _resident(b_s_t.shape), _resident(w_out.shape)],
        out_specs=pl.BlockSpec((tm, d), row),
        out_shape=jax.ShapeDtypeStruct((t, d), F32),
        scratch_shapes=[pltpu.VMEM((tm, width), F32), pltpu.VMEM((tm, width), BF16)],
        compiler_params=_params(),
        name="gmlp",
    )(h, g, w_in, b_in, v_gain, w_s, b_s_t, w_out)


def _rwkv_in_kernel(*refs, tiles_per_seq, with_vres):
    (h_ref, g_ref, mu_ref, wr_ref, wk_ref, wv_ref, w0_ref, w1_ref, w2_ref, a0_ref, a1_ref, a2_ref,
     g1_ref, g2_ref, kk_ref, ka_ref, rk_ref, hsum_ref) = refs[:18]
    if with_vres:
        vf_ref, v0_ref, v1_ref, v2_ref = refs[18:22]
        outs = refs[22:]
    else:
        outs = refs[18:]
    r_o, lw_o, k_o, v_o, an_o, bn_o, gate_o, bonus_o, carry = outs
    tm = h_ref.shape[0]

    @pl.when(pl.program_id(0) % tiles_per_seq == 0)
    def _():
        carry[...] = jnp.zeros_like(carry)

    hn = _rms(h_ref[...], g_ref[2:3, :])
    first = lax.broadcasted_iota(jnp.int32, (tm, 1), 0) == 0
    prev = jnp.where(first, carry[0:1, :], pltpu.roll(hn, 1, axis=0))
    carry[0:1, :] = hn[tm - 1:tm, :]
    dx = prev - hn

    def lerp(i):
        return (hn + dx * mu_ref[i:i + 1, :]).astype(BF16)

    x_v = lerp(2)
    r = _dot(lerp(0), wr_ref[...])
    k = _dot(lerp(1), wk_ref[...])
    v = _dot(x_v, wv_ref[...])
    zw = w0_ref[...] + _dot(jnp.tanh(_dot(lerp(3), w1_ref[...])).astype(BF16), w2_ref[...])
    lw_o[...] = -DECAY_SCALE * jax.nn.sigmoid(zw)
    a = jax.nn.sigmoid(a0_ref[...] + _dot(_dot(lerp(4), a1_ref[...]).astype(BF16), a2_ref[...]))
    gate_o[...] = _dot(jax.nn.sigmoid(_dot(lerp(5), g1_ref[...])).astype(BF16), g2_ref[...])

    kk = k * kk_ref[...]
    ss = _head_sum(kk * kk, hsum_ref[...])
    kk = kk * lax.rsqrt(jnp.maximum(ss, KK_EPS_SQ))
    k = k * (1.0 + (a - 1.0) * ka_ref[...])
    if with_vres:
        mix = jax.nn.sigmoid(v0_ref[...] + _dot(_dot(x_v, v1_ref[...]).astype(BF16), v2_ref[...]))
        v = v + (vf_ref[...] - v) * mix
    r_o[...] = r
    k_o[...] = k
    v_o[...] = v
    an_o[...] = -kk
    bn_o[...] = kk * a
    bonus_o[...] = _head_sum(r * k * rk_ref[...], hsum_ref[...]) * v


def _rwkv_in(h, g, mu, wr, wk, wv, w0, w1, w2, a0, a1, a2, g1, g2, k_k, k_a, r_k, hsum,
             vres, *, seq, tm=256):
    t, d = h.shape
    tm = min(tm, seq)
    row = lambda i: (i, 0)
    tile = pl.BlockSpec((tm, d), row)
    args = [h, g, mu, wr, wk, wv, w0, w1, w2, a0, a1, a2, g1, g2, k_k, k_a, r_k, hsum]
    in_specs = [tile] + [_resident(x.shape) for x in args[1:]]
    if vres is not None:
        v_first, v0, v1, v2 = vres
        args += [v_first, v0, v1, v2]
        in_specs += [tile, _resident(v0.shape), _resident(v1.shape), _resident(v2.shape)]
    kern = functools.partial(_rwkv_in_kernel, tiles_per_seq=seq // tm, with_vres=vres is not None)
    return pl.pallas_call(
        kern,
        grid=(t // tm,),
        in_specs=in_specs,
        out_specs=[tile] * 8,
        out_shape=[jax.ShapeDtypeStruct((t, d), F32)] * 8,
        scratch_shapes=[pltpu.VMEM((8, d), F32)],
        compiler_params=_params(),
        name="rwkv_in",
    )(*args)


def _wkv_kernel(r_ref, lw_ref, k_ref, v_ref, a_ref, b_ref, y_ref, s_scr, *, steps_per_seq):
    L = CHUNK
    rows, d = r_ref.shape
    n_chunks = rows // L

    @pl.when(pl.program_id(0) % steps_per_seq == 0)
    def _():
        s_scr[...] = jnp.zeros_like(s_scr)

    lw = lw_ref[...]
    ri = lax.broadcasted_iota(jnp.int32, (rows, rows), 0)
    cj = lax.broadcasted_iota(jnp.int32, (rows, rows), 1)
    tri = jnp.where((cj <= ri) & (cj // L == ri // L), 1.0, 0.0).astype(BF16)
    hi = lw.astype(BF16)
    rest = lw - hi.astype(F32)
    mid = rest.astype(BF16)
    low = (rest - mid.astype(F32)).astype(BF16)
    cum = _dot(tri, hi) + _dot(tri, mid) + _dot(tri, low)
    last = [cum[(c + 1) * L - 1:(c + 1) * L, :] for c in range(n_chunks)]
    cum_last = jnp.concatenate([jnp.broadcast_to(x, (L, d)) for x in last], axis=0)
    p_in = jnp.exp(cum)
    p_inv = jnp.exp(-cum)
    p_ex = jnp.exp(cum - lw)
    p_dec = jnp.exp(cum_last - cum)
    p_last = [jnp.exp(x) for x in last]

    r = r_ref[...]
    k = k_ref[...]
    a = a_ref[...]
    b = b_ref[...]
    rt = (r * p_in).astype(BF16)
    at = (a * p_ex).astype(BF16)
    kt = (k * p_inv).astype(BF16)
    bt = (b * p_inv).astype(BF16)
    kd = (k * p_dec).astype(BF16)
    bd = (b * p_dec).astype(BF16)
    vb = v_ref[...].astype(BF16)

    head0 = lax.broadcasted_iota(jnp.int32, (1, LANES), 1) < HEAD_DIM
    ti = lax.broadcasted_iota(jnp.int32, (L, LANES), 0)
    sj = lax.broadcasted_iota(jnp.int32, (L, LANES), 1) % HEAD_DIM
    strict = sj < ti
    incl = sj <= ti
    eye = jnp.where(sj == ti, 1.0, 0.0)

    def stack(x):
        zero = jnp.zeros_like(x)
        return jnp.concatenate([jnp.where(head0, x, zero), jnp.where(head0, zero, x)], axis=0)

    n_pairs = d // LANES
    probs = [(slice(c * L, (c + 1) * L), slice(p * LANES, (p + 1) * LANES))
             for c in range(n_chunks) for p in range(n_pairs)]
    qs = range(len(probs))
    g = [_dot_nt(jnp.concatenate([at[q], rt[q]], axis=0),
                 jnp.concatenate([stack(bt[q]), stack(kt[q])], axis=0)) for q in probs]
    a_ab = [jnp.where(strict, x[:L, :LANES], 0.0) for x in g]
    a_ak = [jnp.where(strict, x[:L, LANES:], 0.0).astype(BF16) for x in g]
    a_rb = [jnp.where(incl, x[L:, :LANES], 0.0).astype(BF16) for x in g]
    a_rk = [jnp.where(incl, x[L:, LANES:], 0.0).astype(BF16) for x in g]
    v2 = [stack(vb[q]) for q in probs]
    avy = [_dot(jnp.concatenate([a_ak[q], a_rk[q]], axis=0), v2[q]) for q in qs]
    minv = [eye + x for x in a_ab]
    apow = [x.astype(BF16) for x in a_ab]
    apow = [_dot(x, stack(x)).astype(BF16) for x in apow]
    for i in range(4):
        res = [_dot(apow[q], jnp.concatenate([stack(minv[q].astype(BF16)), stack(apow[q])], axis=1))
               for q in qs]
        minv = [minv[q] + res[q][:, :LANES] for q in qs]
        apow = [x[:, LANES:].astype(BF16) for x in res]
    minv = [(minv[q] + _dot(apow[q], stack(minv[q].astype(BF16)))).astype(BF16) for q in qs]
    mw = [_dot(minv[q], jnp.concatenate([stack(at[probs[q]]), stack(avy[q][:L].astype(BF16))], axis=1))
          for q in qs]

    s = [s_scr[p] for p in range(n_pairs)]
    for c in range(n_chunks):
        cq = [c * n_pairs + p for p in range(n_pairs)]
        sb = [x.astype(BF16) for x in s]
        us = [_dot_nt(jnp.concatenate([mw[q][:, :LANES].astype(BF16), rt[probs[q]]], axis=0), sb[p])
              for p, q in enumerate(cq)]
        u2 = [stack((us[p][:L] + mw[q][:, LANES:]).astype(BF16)) for p, q in enumerate(cq)]
        for p, q in enumerate(cq):
            y_ref[probs[q]] = us[p][L:] + _dot(a_rb[q], u2[p]) + avy[q][L:]
        upd = [_dot_tn(jnp.concatenate([u2[p], v2[q]], axis=0),
                       jnp.concatenate([stack(bd[probs[q]]), stack(kd[probs[q]])], axis=0))
               for p, q in enumerate(cq)]
        s = [s[p] * p_last[c][:, probs[q][1]] + upd[p] for p, q in enumerate(cq)]
    for p in range(n_pairs):
        s_scr[p] = s[p]


def _wkv(r, lw, k, v, an, bn, *, seq, n_chunks=4):
    t, d = r.shape
    rows = n_chunks * CHUNK
    row = lambda i: (i, 0)
    tile = pl.BlockSpec((rows, d), row)
    return pl.pallas_call(
        functools.partial(_wkv_kernel, steps_per_seq=seq // rows),
        grid=(t // rows,),
        in_specs=[tile] * 6,
        out_specs=tile,
        out_shape=jax.ShapeDtypeStruct((t, d), F32),
        scratch_shapes=[pltpu.VMEM((d // LANES, LANES, LANES), F32)],
        compiler_params=_params(),
        name="wkv",
    )(r, lw, k, v, an, bn)


def _rwkv_out_kernel(h_ref, y_ref, bonus_ref, gate_ref, g_ref, lnx_ref, hsum_ref, wout_ref, o_ref):
    y = y_ref[...]
    yc = y - _head_sum(y, hsum_ref[...]) * (1.0 / HEAD_DIM)
    var = _head_sum(yc * yc, hsum_ref[...]) * (1.0 / HEAD_DIM)
    yn = yc * lax.rsqrt(var + GN_EPS) * lnx_ref[0:1, :] + lnx_ref[1:2, :]
    out = _dot(((yn + bonus_ref[...]) * gate_ref[...]).astype(BF16), wout_ref[...])
    o_ref[...] = h_ref[...] + _rms(out, g_ref[3:4, :])


def _rwkv_out(h, y, bonus, gate, g, lnx, hsum, w_out, *, tm=1024):
    t, d = h.shape
    tm = min(tm, t)
    tile = pl.BlockSpec((tm, d), lambda i: (i, 0))
    return pl.pallas_call(
        _rwkv_out_kernel,
        grid=(t // tm,),
        in_specs=[tile] * 4 + [_resident(g.shape), _resident(lnx.shape), _resident(hsum.shape),
                               _resident(w_out.shape)],
        out_specs=tile,
        out_shape=jax.ShapeDtypeStruct((t, d), F32),
        compiler_params=_params(),
        name="rwkv_out",
    )(h, y, bonus, gate, g, lnx, hsum, w_out)


def kernel(x, p, norm_g, ffn_w13, ffn_w2, ple_w_gate, ple_w_proj, a_w_in, a_b_in, a_v_gain, a_w_s, a_b_s, a_w_out, b_mu, b_w_in, b_w0, b_w1, b_w2, b_a0, b_a1, b_a2, b_g1, b_g2, b_k_k, b_k_a, b_r_k, b_lnx, b_w_out, b_v0, b_v1, b_v2):
    bsz, seq, d = x.shape
    depth = norm_g.shape[0]
    t = bsz * seq
    bf = lambda w: w.astype(BF16)
    row = lambda w: w.reshape(1, -1)

    head = jnp.arange(MXU_WIDTH) // HEAD_DIM
    hsum = (head[:, None] == head[None, :]).astype(BF16)

    h = x.reshape(t, d)
    v_first = None
    for i in range(depth):
        g = norm_g[i]
        j = i // 2
        h = _ffn(h, g, bf(ffn_w13[i, 0]), bf(ffn_w2[i, 0]), g_in=0, g_out=1)
        if i % 2 == 0:
            h = _gmlp(h, g, bf(a_w_in[j]), row(a_b_in[j]), row(a_v_gain[j]), a_w_s[j],
                      a_b_s[j].T, bf(a_w_out[j]))
        else:
            vres = None
            if v_first is not None:
                vres = (v_first, row(b_v0[j - 1]), bf(b_v1[j - 1]), bf(b_v2[j - 1]))
            r, lw, k, v, an, bn, gate, bonus = _rwkv_in(
                h, g, b_mu[j], bf(b_w_in[j, 0]), bf(b_w_in[j, 1]), bf(b_w_in[j, 2]),
                row(b_w0[j]), bf(b_w1[j]), bf(b_w2[j]), row(b_a0[j]), bf(b_a1[j]), bf(b_a2[j]),
                bf(b_g1[j]), bf(b_g2[j]), row(b_k_k[j]), row(b_k_a[j]), row(b_r_k[j]), hsum,
                vres, seq=seq)
            if v_first is None:
                v_first = v
            y = _wkv(r, lw, k, v, an, bn, seq=seq)
            h = _rwkv_out(h, y, bonus, gate, g, b_lnx[j], hsum, bf(b_w_out[j]))
        h = _ffn(h, g, bf(ffn_w13[i, 1]), bf(ffn_w2[i, 1]), g_in=4, g_out=5,
                 ple=(p[i].reshape(t, -1), bf(ple_w_gate[i]), bf(ple_w_proj[i])))
    return h.reshape(bsz, seq, d)
```

```python
import functools
import math

import jax
import jax.numpy as jnp
from jax import lax
from jax.experimental import pallas as pl
from jax.experimental.pallas import tpu as pltpu

F32 = jnp.float32
BF16 = jnp.bfloat16

RMS_EPS = 1e-6
LN_EPS = 1e-5
GN_EPS = 64e-5
KK_EPS_SQ = 1e-24
DECAY_SCALE = math.exp(-0.5)

HEAD_DIM = 64
GMLP_BLOCK = 128
GMLP_GROUPS = 8
CHUNK = 64
LANES = 128
MXU_WIDTH = 256

V7X_VMEM_LIMIT_BYTES = 56 * 1024 * 1024


def _dot(a, b):
    return jnp.dot(a, b, preferred_element_type=F32)


def _dot_nt(a, b):
    return lax.dot_general(a, b, (((1,), (1,)), ((), ())), preferred_element_type=F32)


def _dot_tn(a, b):
    return lax.dot_general(a, b, (((0,), (0,)), ((), ())), preferred_element_type=F32)


def _rms(x, g):
    return x * lax.rsqrt(jnp.mean(x * x, axis=-1, keepdims=True) + RMS_EPS) * g


def _head_sum(x, hsum):
    w = hsum.shape[0]
    xb = x.astype(BF16)
    return jnp.concatenate([_dot(xb[:, i:i + w], hsum) for i in range(0, x.shape[1], w)], axis=1)


def _gelu_tanh(x):
    c = -2.0 * math.sqrt(2.0 / math.pi) * math.log2(math.e)
    z = x * (x * x * (c * 0.044715) + c)
    return x / (1.0 + jnp.exp2(z))


def _resident(shape):
    nd = len(shape)
    return pl.BlockSpec(shape, lambda *_: (0,) * nd, pipeline_mode=pl.Buffered(1))


def _params(n_axes=1):
    return pltpu.CompilerParams(
        dimension_semantics=("arbitrary",) * n_axes,
        vmem_limit_bytes=V7X_VMEM_LIMIT_BYTES)


def _ffn_kernel(*refs, d_ff, f_chunk, g_in, g_out, with_ple):
    if with_ple:
        h_ref, g_ref, w13_ref, w2_ref, p_ref, wg_ref, wp_ref, o_ref, acc_ref = refs
    else:
        h_ref, g_ref, w13_ref, w2_ref, o_ref, acc_ref = refs
    h = h_ref[...]
    xn = _rms(h, g_ref[g_in:g_in + 1, :]).astype(BF16)
    for j in range(d_ff // f_chunk):
        lo = j * f_chunk
        gate = _dot(xn, w13_ref[:, lo:lo + f_chunk].astype(BF16))
        up = _dot(xn, w13_ref[:, d_ff + lo:d_ff + lo + f_chunk].astype(BF16))
        act = (gate * jax.nn.sigmoid(gate) * up).astype(BF16)
        part = _dot(act, w2_ref[lo:lo + f_chunk, :].astype(BF16))
        if j == 0:
            acc_ref[...] = part
        else:
            acc_ref[...] += part
    h = h + 0.5 * _rms(acc_ref[...], g_ref[g_out:g_out + 1, :])
    if with_ple:
        hn = _rms(h, g_ref[6:7, :]).astype(BF16)
        gate = jax.nn.sigmoid(_dot(hn, wg_ref[...]))
        proj = _dot(p_ref[...].astype(BF16), wp_ref[...])
        h = h + _rms(gate * proj, g_ref[7:8, :])
    o_ref[...] = h


def _ffn(h, g, w13, w2, *, g_in, g_out, ple=None, tm=512, f_chunk=256):
    t, d = h.shape
    d_ff = w2.shape[0]
    tm = min(tm, t)
    row = lambda i: (i, 0)
    in_specs = [pl.BlockSpec((tm, d), row), _resident(g.shape),
                _resident(w13.shape), _resident(w2.shape)]
    args = [h, g, w13, w2]
    if ple is not None:
        p, wg, wp = ple
        in_specs += [pl.BlockSpec((tm, p.shape[1]), row), _resident(wg.shape), _resident(wp.shape)]
        args += [p, wg, wp]
    kern = functools.partial(_ffn_kernel, d_ff=d_ff, f_chunk=f_chunk, g_in=g_in, g_out=g_out,
                             with_ple=ple is not None)
    return pl.pallas_call(
        kern,
        grid=(t // tm,),
        in_specs=in_specs,
        out_specs=pl.BlockSpec((tm, d), row),
        out_shape=jax.ShapeDtypeStruct((t, d), F32),
        scratch_shapes=[pltpu.VMEM((tm, d), F32)],
        compiler_params=_params(),
        name="ffn_ple" if ple is not None else "ffn",
    )(*args)


def _gmlp_kernel(h_ref, g_ref, win_ref, bin_ref, vgain_ref, ws_ref, bst_ref, wout_ref,
                 o_ref, v_scr, y_scr, *, width):
    tm = h_ref.shape[0]
    gw = width // GMLP_GROUPS
    h = h_ref[...]
    hn = _rms(h, g_ref[2:3, :]).astype(BF16)
    for g in range(GMLP_GROUPS):
        lo = width + g * gw
        v_scr[:, g * gw:(g + 1) * gw] = _gelu_tanh(_dot(hn, win_ref[:, lo:lo + gw]) + bin_ref[:, lo:lo + gw])
    v = v_scr[...]
    mean = jnp.mean(v, axis=-1, keepdims=True)
    vc = v - mean
    var = jnp.mean(vc * vc, axis=-1, keepdims=True)
    v_scr[...] = vc * lax.rsqrt(var + LN_EPS) * vgain_ref[...]
    ci = lax.broadcasted_iota(jnp.int32, (GMLP_BLOCK, GMLP_BLOCK), 0) // CHUNK
    cj = lax.broadcasted_iota(jnp.int32, (GMLP_BLOCK, GMLP_BLOCK), 1) // CHUNK
    causal = cj <= ci
    for g in range(GMLP_GROUPS):
        ws = jnp.where(causal, ws_ref[g], 0.0).astype(BF16)
        bias = bst_ref[:, g:g + 1]
        u = _gelu_tanh(_dot(hn, win_ref[:, g * gw:(g + 1) * gw]) + bin_ref[:, g * gw:(g + 1) * gw])
        for c in range(tm // GMLP_BLOCK):
            rows = slice(c * GMLP_BLOCK, (c + 1) * GMLP_BLOCK)
            s = _dot(ws, v_scr[rows, g * gw:(g + 1) * gw].astype(BF16)) + bias
            y_scr[rows, g * gw:(g + 1) * gw] = (u[rows, :] * s).astype(BF16)
    m = _dot(y_scr[...], wout_ref[...])
    o_ref[...] = h + _rms(m, g_ref[3:4, :])


def _gmlp(h, g, w_in, b_in, v_gain, w_s, b_s_t, w_out, *, tm=512):
    t, d = h.shape
    width = w_out.shape[0]
    tm = min(tm, t)
    row = lambda i: (i, 0)
    return pl.pallas_call(
        functools.partial(_gmlp_kernel, width=width),
        grid=(t // tm,),
        in_specs=[pl.BlockSpec((tm, d), row), _resident(g.shape), _resident(w_in.shape),
                  _resident(b_in.shape), _resident(v_gain.shape), _resident(w_s.shape),
                  _resident(b_s_t.shape), _resident(w_out.shape)],
        out_specs=pl.BlockSpec((tm, d), row),
        out_shape=jax.ShapeDtypeStruct((t, d), F32),
        scratch_shapes=[pltpu.VMEM((tm, width), F32), pltpu.VMEM((tm, width), BF16)],
        compiler_params=_params(),
        name="gmlp",
    )(h, g, w_in, b_in, v_gain, w_s, b_s_t, w_out)


def _rwkv_in_kernel(*refs, tiles_per_seq, with_vres):
    (h_ref, g_ref, mu_ref, wr_ref, wk_ref, wv_ref, w0_ref, w1_ref, w2_ref, a0_ref, a1_ref, a2_ref,
     g1_ref, g2_ref, kk_ref, ka_ref, rk_ref, hsum_ref) = refs[:18]
    if with_vres:
        vf_ref, v0_ref, v1_ref, v2_ref = refs[18:22]
        outs = refs[22:]
    else:
        outs = refs[18:]
    r_o, lw_o, k_o, v_o, an_o, bn_o, gate_o, bonus_o, carry = outs
    tm = h_ref.shape[0]

    @pl.when(pl.program_id(0) % tiles_per_seq == 0)
    def _():
        carry[...] = jnp.zeros_like(carry)

    hn = _rms(h_ref[...], g_ref[2:3, :])
    first = lax.broadcasted_iota(jnp.int32, (tm, 1), 0) == 0
    prev = jnp.where(first, carry[0:1, :], pltpu.roll(hn, 1, axis=0))
    carry[0:1, :] = hn[tm - 1:tm, :]
    dx = prev - hn

    def lerp(i):
        return (hn + dx * mu_ref[i:i + 1, :]).astype(BF16)

    x_v = lerp(2)
    r = _dot(lerp(0), wr_ref[...])
    k = _dot(lerp(1), wk_ref[...])
    v = _dot(x_v, wv_ref[...])
    zw = w0_ref[...] + _dot(jnp.tanh(_dot(lerp(3), w1_ref[...])).astype(BF16), w2_ref[...])
    lw_o[...] = -DECAY_SCALE * jax.nn.sigmoid(zw)
    a = jax.nn.sigmoid(a0_ref[...] + _dot(_dot(lerp(4), a1_ref[...]).astype(BF16), a2_ref[...]))
    gate_o[...] = _dot(jax.nn.sigmoid(_dot(lerp(5), g1_ref[...])).astype(BF16), g2_ref[...]).astype(BF16)

    kk = k * kk_ref[...]
    ss = _head_sum(kk * kk, hsum_ref[...])
    kk = kk * lax.rsqrt(jnp.maximum(ss, KK_EPS_SQ))
    k = k * (1.0 + (a - 1.0) * ka_ref[...])
    if with_vres:
        mix = jax.nn.sigmoid(v0_ref[...] + _dot(_dot(x_v, v1_ref[...]).astype(BF16), v2_ref[...]))
        v = v + (vf_ref[...].astype(F32) - v) * mix
    r_o[...] = r.astype(BF16)
    k_o[...] = k.astype(BF16)
    v_o[...] = v.astype(BF16)
    an_o[...] = (-kk).astype(BF16)
    bn_o[...] = (kk * a).astype(BF16)
    bonus_o[...] = (_head_sum(r * k * rk_ref[...], hsum_ref[...]) * v).astype(BF16)


def _rwkv_in(h, g, mu, wr, wk, wv, w0, w1, w2, a0, a1, a2, g1, g2, k_k, k_a, r_k, hsum,
             vres, *, seq, tm=512):
    t, d = h.shape
    tm = min(tm, seq)
    row = lambda i: (i, 0)
    tile = pl.BlockSpec((tm, d), row)
    args = [h, g, mu, wr, wk, wv, w0, w1, w2, a0, a1, a2, g1, g2, k_k, k_a, r_k, hsum]
    in_specs = [tile] + [_resident(x.shape) for x in args[1:]]
    if vres is not None:
        v_first, v0, v1, v2 = vres
        args += [v_first, v0, v1, v2]
        in_specs += [tile, _resident(v0.shape), _resident(v1.shape), _resident(v2.shape)]
    kern = functools.partial(_rwkv_in_kernel, tiles_per_seq=seq // tm, with_vres=vres is not None)
    return pl.pallas_call(
        kern,
        grid=(t // tm,),
        in_specs=in_specs,
        out_specs=[tile] * 8,
        out_shape=[jax.ShapeDtypeStruct((t, d), F32 if i == 1 else BF16) for i in range(8)],
        scratch_shapes=[pltpu.VMEM((8, d), F32)],
        compiler_params=_params(),
        name="rwkv_in",
    )(*args)


def _wkv_kernel(r_ref, lw_ref, k_ref, v_ref, a_ref, b_ref, y_ref, s_scr, *, steps_per_seq):
    L = CHUNK
    rows, d = r_ref.shape
    n_chunks = rows // L

    @pl.when(pl.program_id(0) % steps_per_seq == 0)
    def _():
        s_scr[...] = jnp.zeros_like(s_scr)

    lw = lw_ref[...]
    ri = lax.broadcasted_iota(jnp.int32, (rows, rows), 0)
    cj = lax.broadcasted_iota(jnp.int32, (rows, rows), 1)
    tri = jnp.where((cj <= ri) & (cj // L == ri // L), 1.0, 0.0).astype(BF16)
    hi = lw.astype(BF16)
    rest = lw - hi.astype(F32)
    mid = rest.astype(BF16)
    low = (rest - mid.astype(F32)).astype(BF16)
    cum = _dot(tri, hi) + _dot(tri, mid) + _dot(tri, low)
    last = [cum[(c + 1) * L - 1:(c + 1) * L, :] for c in range(n_chunks)]
    cum_last = jnp.concatenate([jnp.broadcast_to(x, (L, d)) for x in last], axis=0)
    p_in = jnp.exp(cum)
    p_inv = jnp.exp(-cum)
    p_ex = jnp.exp(cum - lw)
    p_dec = jnp.exp(cum_last - cum)
    p_last = [jnp.exp(x) for x in last]

    r = r_ref[...].astype(F32)
    k = k_ref[...].astype(F32)
    a = a_ref[...].astype(F32)
    b = b_ref[...].astype(F32)
    rt = (r * p_in).astype(BF16)
    at = (a * p_ex).astype(BF16)
    kt = (k * p_inv).astype(BF16)
    bt = (b * p_inv).astype(BF16)
    kd = (k * p_dec).astype(BF16)
    bd = (b * p_dec).astype(BF16)
    vb = v_ref[...]

    head0 = lax.broadcasted_iota(jnp.int32, (1, LANES), 1) < HEAD_DIM
    ti = lax.broadcasted_iota(jnp.int32, (L, LANES), 0)
    sj = lax.broadcasted_iota(jnp.int32, (L, LANES), 1) % HEAD_DIM
    strict = sj < ti
    incl = sj <= ti
    eye = jnp.where(sj == ti, 1.0, 0.0)

    def stack(x):
        zero = jnp.zeros_like(x)
        return jnp.concatenate([jnp.where(head0, x, zero), jnp.where(head0, zero, x)], axis=0)

    n_pairs = d // LANES
    probs = [(slice(c * L, (c + 1) * L), slice(p * LANES, (p + 1) * LANES))
             for c in range(n_chunks) for p in range(n_pairs)]
    qs = range(len(probs))
    g = [_dot_nt(jnp.concatenate([at[q], rt[q]], axis=0),
                 jnp.concatenate([stack(bt[q]), stack(kt[q])], axis=0)) for q in probs]
    a_ab = [jnp.where(strict, x[:L, :LANES], 0.0) for x in g]
    a_ak = [jnp.where(strict, x[:L, LANES:], 0.0).astype(BF16) for x in g]
    a_rb = [jnp.where(incl, x[L:, :LANES], 0.0).astype(BF16) for x in g]
    a_rk = [jnp.where(incl, x[L:, LANES:], 0.0).astype(BF16) for x in g]
    v2 = [stack(vb[q]) for q in probs]
    avy = [_dot(jnp.concatenate([a_ak[q], a_rk[q]], axis=0), v2[q]) for q in qs]
    minv = [eye + x for x in a_ab]
    apow = [x.astype(BF16) for x in a_ab]
    apow = [_dot(x, stack(x)).astype(BF16) for x in apow]
    for i in range(4):
        res = [_dot(apow[q], jnp.concatenate([stack(minv[q].astype(BF16)), stack(apow[q])], axis=1))
               for q in qs]
        minv = [minv[q] + res[q][:, :LANES] for q in qs]
        apow = [x[:, LANES:].astype(BF16) for x in res]
    minv = [(minv[q] + _dot(apow[q], stack(minv[q].astype(BF16)))).astype(BF16) for q in qs]
    mw = [_dot(minv[q], jnp.concatenate([stack(at[probs[q]]), stack(avy[q][:L].astype(BF16))], axis=1))
          for q in qs]

    s = [s_scr[p] for p in range(n_pairs)]
    for c in range(n_chunks):
        cq = [c * n_pairs + p for p in range(n_pairs)]
        sb = [x.astype(BF16) for x in s]
        us = [_dot_nt(jnp.concatenate([mw[q][:, :LANES].astype(BF16), rt[probs[q]]], axis=0), sb[p])
              for p, q in enumerate(cq)]
        u2 = [stack((us[p][:L] + mw[q][:, LANES:]).astype(BF16)) for p, q in enumerate(cq)]
        for p, q in enumerate(cq):
            y_ref[probs[q]] = (us[p][L:] + _dot(a_rb[q], u2[p]) + avy[q][L:]).astype(BF16)
        upd = [_dot_tn(jnp.concatenate([u2[p], v2[q]], axis=0),
                       jnp.concatenate([stack(bd[probs[q]]), stack(kd[probs[q]])], axis=0))
               for p, q in enumerate(cq)]
        s = [s[p] * p_last[c][:, probs[q][1]] + upd[p] for p, q in enumerate(cq)]
    for p in range(n_pairs):
        s_scr[p] = s[p]


def _wkv(r, lw, k, v, an, bn, *, seq, n_chunks=4):
    t, d = r.shape
    rows = n_chunks * CHUNK
    row = lambda i: (i, 0)
    tile = pl.BlockSpec((rows, d), row)
    return pl.pallas_call(
        functools.partial(_wkv_kernel, steps_per_seq=seq // rows),
        grid=(t // rows,),
        in_specs=[tile] * 6,
        out_specs=tile,
        out_shape=jax.ShapeDtypeStruct((t, d), BF16),
        scratch_shapes=[pltpu.VMEM((d // LANES, LANES, LANES), F32)],
        compiler_params=_params(),
        name="wkv",
    )(r, lw, k, v, an, bn)


def _rwkv_out_kernel(h_ref, y_ref, bonus_ref, gate_ref, g_ref, lnx_ref, hsum_ref, wout_ref, o_ref):
    y = y_ref[...].astype(F32)
    yc = y - _head_sum(y, hsum_ref[...]) * (1.0 / HEAD_DIM)
    var = _head_sum(yc * yc, hsum_ref[...]) * (1.0 / HEAD_DIM)
    yn = yc * lax.rsqrt(var + GN_EPS) * lnx_ref[0:1, :] + lnx_ref[1:2, :]
    out = _dot(((yn + bonus_ref[...]) * gate_ref[...]).astype(BF16), wout_ref[...])
    o_ref[...] = h_ref[...] + _rms(out, g_ref[3:4, :])


def _rwkv_out(h, y, bonus, gate, g, lnx, hsum, w_out, *, tm=1024):
    t, d = h.shape
    tm = min(tm, t)
    tile = pl.BlockSpec((tm, d), lambda i: (i, 0))
    return pl.pallas_call(
        _rwkv_out_kernel,
        grid=(t // tm,),
        in_specs=[tile] * 4 + [_resident(g.shape), _resident(lnx.shape), _resident(hsum.shape),
                               _resident(w_out.shape)],
        out_specs=tile,
        out_shape=jax.ShapeDtypeStruct((t, d), F32),
        compiler_params=_params(),
        name="rwkv_out",
    )(h, y, bonus, gate, g, lnx, hsum, w_out)


def kernel(x, p, norm_g, ffn_w13, ffn_w2, ple_w_gate, ple_w_proj, a_w_in, a_b_in, a_v_gain, a_w_s, a_b_s, a_w_out, b_mu, b_w_in, b_w0, b_w1, b_w2, b_a0, b_a1, b_a2, b_g1, b_g2, b_k_k, b_k_a, b_r_k, b_lnx, b_w_out, b_v0, b_v1, b_v2):
    bsz, seq, d = x.shape
    depth = norm_g.shape[0]
    t = bsz * seq
    bf = lambda w: w.astype(BF16)
    row = lambda w: w.reshape(1, -1)

    head = jnp.arange(MXU_WIDTH) // HEAD_DIM
    hsum = (head[:, None] == head[None, :]).astype(BF16)

    h = x.reshape(t, d)
    v_first = None
    for i in range(depth):
        g = norm_g[i]
        j = i // 2
        h = _ffn(h, g, ffn_w13[i, 0], ffn_w2[i, 0], g_in=0, g_out=1)
        if i % 2 == 0:
            h = _gmlp(h, g, bf(a_w_in[j]), row(a_b_in[j]), row(a_v_gain[j]), a_w_s[j],
                      a_b_s[j].T, bf(a_w_out[j]))
        else:
            vres = None
            if v_first is not None:
                vres = (v_first, row(b_v0[j - 1]), bf(b_v1[j - 1]), bf(b_v2[j - 1]))
            r, lw, k, v, an, bn, gate, bonus = _rwkv_in(
                h, g, b_mu[j], bf(b_w_in[j, 0]), bf(b_w_in[j, 1]), bf(b_w_in[j, 2]),
                row(b_w0[j]), bf(b_w1[j]), bf(b_w2[j]), row(b_a0[j]), bf(b_a1[j]), bf(b_a2[j]),
                bf(b_g1[j]), bf(b_g2[j]), row(b_k_k[j]), row(b_k_a[j]), row(b_r_k[j]), hsum,
                vres, seq=seq)
            if v_first is None:
                v_first = v
            y = _wkv(r, lw, k, v, an, bn, seq=seq)
            h = _rwkv_out(h, y, bonus, gate, g, b_lnx[j], hsum, bf(b_w_out[j]))
        h = _ffn(h, g, ffn_w13[i, 1], ffn_w2[i, 1], g_in=4, g_out=5,
                 ple=(p[i].reshape(t, -1), bf(ple_w_gate[i]), bf(ple_w_proj[i])))
    return h.reshape(bsz, seq, d)
```

```python
import functools
import math

import jax
import jax.numpy as jnp
from jax import lax
from jax.experimental import pallas as pl
from jax.experimental.pallas import tpu as pltpu

F32 = jnp.float32
BF16 = jnp.bfloat16

RMS_EPS = 1e-6
LN_EPS = 1e-5
GN_EPS = 64e-5
KK_EPS_SQ = 1e-24
DECAY_SCALE = math.exp(-0.5)

HEAD_DIM = 64
GMLP_BLOCK = 128
GMLP_GROUPS = 8
CHUNK = 64
LANES = 128
MXU_WIDTH = 256

V7X_VMEM_LIMIT_BYTES = 56 * 1024 * 1024


def _dot(a, b):
    return jnp.dot(a, b, preferred_element_type=F32)


def _dot_nt(a, b):
    return lax.dot_general(a, b, (((1,), (1,)), ((), ())), preferred_element_type=F32)


def _dot_tn(a, b):
    return lax.dot_general(a, b, (((0,), (0,)), ((), ())), preferred_element_type=F32)


def _rms(x, g):
    return x * lax.rsqrt(jnp.mean(x * x, axis=-1, keepdims=True) + RMS_EPS) * g


def _head_sum(x, hsum):
    w = hsum.shape[0]
    xb = x.astype(BF16)
    return jnp.concatenate([_dot(xb[:, i:i + w], hsum) for i in range(0, x.shape[1], w)], axis=1)


def _gelu_tanh(x):
    c = -2.0 * math.sqrt(2.0 / math.pi) * math.log2(math.e)
    z = x * (x * x * (c * 0.044715) + c)
    return x / (1.0 + jnp.exp2(z))


def _resident(shape):
    nd = len(shape)
    return pl.BlockSpec(shape, lambda *_: (0,) * nd, pipeline_mode=pl.Buffered(1))


def _layer_block(arr, *idx):
    rest = arr.shape[len(idx):]
    return pl.BlockSpec((None,) * len(idx) + rest, lambda *_: idx + (0,) * len(rest),
                        pipeline_mode=pl.Buffered(1))


def _params(n_axes=1):
    return pltpu.CompilerParams(
        dimension_semantics=("arbitrary",) * n_axes,
        vmem_limit_bytes=V7X_VMEM_LIMIT_BYTES)


def _ffn_kernel(*refs, d_ff, f_chunk, g_in, g_out, with_ple):
    if with_ple:
        h_ref, g_ref, w13_ref, w2_ref, p_ref, wg_ref, wp_ref, o_ref, acc_ref = refs
    else:
        h_ref, g_ref, w13_ref, w2_ref, o_ref, acc_ref = refs
    h = h_ref[...]
    xn = _rms(h, g_ref[g_in:g_in + 1, :]).astype(BF16)
    for j in range(d_ff // f_chunk):
        lo = j * f_chunk
        gate = _dot(xn, w13_ref[:, lo:lo + f_chunk].astype(BF16))
        up = _dot(xn, w13_ref[:, d_ff + lo:d_ff + lo + f_chunk].astype(BF16))
        act = (gate * jax.nn.sigmoid(gate) * up).astype(BF16)
        part = _dot(act, w2_ref[lo:lo + f_chunk, :].astype(BF16))
        if j == 0:
            acc_ref[...] = part
        else:
            acc_ref[...] += part
    h = h + 0.5 * _rms(acc_ref[...], g_ref[g_out:g_out + 1, :])
    if with_ple:
        hn = _rms(h, g_ref[6:7, :]).astype(BF16)
        gate = jax.nn.sigmoid(_dot(hn, wg_ref[...]))
        proj = _dot(p_ref[...].astype(BF16), wp_ref[...])
        h = h + _rms(gate * proj, g_ref[7:8, :])
    o_ref[...] = h


def _ffn(h, g, w13_all, w2_all, layer, half, *, g_in, g_out, ple=None, tm=512, f_chunk=256):
    t, d = h.shape
    d_ff = w2_all.shape[2]
    tm = min(tm, t)
    row = lambda i: (i, 0)
    in_specs = [pl.BlockSpec((tm, d), row), _resident(g.shape),
                _layer_block(w13_all, layer, half), _layer_block(w2_all, layer, half)]
    args = [h, g, w13_all, w2_all]
    if ple is not None:
        p_all, wg, wp = ple
        in_specs += [pl.BlockSpec((None, tm, p_all.shape[2]), lambda i: (layer, i, 0)),
                     _resident(wg.shape), _resident(wp.shape)]
        args += [p_all, wg, wp]
    kern = functools.partial(_ffn_kernel, d_ff=d_ff, f_chunk=f_chunk, g_in=g_in, g_out=g_out,
                             with_ple=ple is not None)
    return pl.pallas_call(
        kern,
        grid=(t // tm,),
        in_specs=in_specs,
        out_specs=pl.BlockSpec((tm, d), row),
        out_shape=jax.ShapeDtypeStruct((t, d), F32),
        scratch_shapes=[pltpu.VMEM((tm, d), F32)],
        compiler_params=_params(),
        name="ffn_ple" if ple is not None else "ffn",
    )(*args)


def _gmlp_kernel(h_ref, g_ref, win_ref, bin_ref, vgain_ref, ws_ref, bst_ref, wout_ref,
                 o_ref, v_scr, y_scr, *, width):
    tm = h_ref.shape[0]
    gw = width // GMLP_GROUPS
    h = h_ref[...]
    hn = _rms(h, g_ref[2:3, :]).astype(BF16)
    for g in range(GMLP_GROUPS):
        lo = width + g * gw
        v_scr[:, g * gw:(g + 1) * gw] = _gelu_tanh(
            _dot(hn, win_ref[:, lo:lo + gw].astype(BF16)) + bin_ref[:, lo:lo + gw])
    v = v_scr[...]
    mean = jnp.mean(v, axis=-1, keepdims=True)
    vc = v - mean
    var = jnp.mean(vc * vc, axis=-1, keepdims=True)
    v_scr[...] = vc * lax.rsqrt(var + LN_EPS) * vgain_ref[...]
    ci = lax.broadcasted_iota(jnp.int32, (GMLP_BLOCK, GMLP_BLOCK), 0) // CHUNK
    cj = lax.broadcasted_iota(jnp.int32, (GMLP_BLOCK, GMLP_BLOCK), 1) // CHUNK
    causal = cj <= ci
    for g in range(GMLP_GROUPS):
        ws = jnp.where(causal, ws_ref[g], 0.0).astype(BF16)
        bias = bst_ref[:, g:g + 1]
        u = _gelu_tanh(
            _dot(hn, win_ref[:, g * gw:(g + 1) * gw].astype(BF16)) + bin_ref[:, g * gw:(g + 1) * gw])
        for c in range(tm // GMLP_BLOCK):
            rows = slice(c * GMLP_BLOCK, (c + 1) * GMLP_BLOCK)
            s = _dot(ws, v_scr[rows, g * gw:(g + 1) * gw].astype(BF16)) + bias
            y_scr[rows, g * gw:(g + 1) * gw] = (u[rows, :] * s).astype(BF16)
    m = _dot(y_scr[...], wout_ref[...].astype(BF16))
    o_ref[...] = h + _rms(m, g_ref[3:4, :])


def _gmlp(h, g, w_in_all, b_in, v_gain, w_s, b_s_t, w_out_all, layer, *, tm=512):
    t, d = h.shape
    width = w_out_all.shape[1]
    tm = min(tm, t)
    row = lambda i: (i, 0)
    return pl.pallas_call(
        functools.partial(_gmlp_kernel, width=width),
        grid=(t // tm,),
        in_specs=[pl.BlockSpec((tm, d), row), _resident(g.shape), _layer_block(w_in_all, layer),
                  _resident(b_in.shape), _resident(v_gain.shape), _resident(w_s.shape),
                  _resident(b_s_t.shape), _layer_block(w_out_all, layer)],
        out_specs=pl.BlockSpec((tm, d), row),
        out_shape=jax.ShapeDtypeStruct((t, d), F32),
        scratch_shapes=[pltpu.VMEM((tm, width), F32), pltpu.VMEM((tm, width), BF16)],
        compiler_params=_params(),
        name="gmlp",
    )(h, g, w_in_all, b_in, v_gain, w_s, b_s_t, w_out_all)


def _rwkv_in_kernel(*refs, tiles_per_seq, with_vres):
    (h_ref, g_ref, mu_ref, wr_ref, wk_ref, wv_ref, w0_ref, w1_ref, w2_ref, a0_ref, a1_ref, a2_ref,
     g1_ref, g2_ref, kk_ref, ka_ref, rk_ref, hsum_ref) = refs[:18]
    if with_vres:
        vf_ref, v0_ref, v1_ref, v2_ref = refs[18:22]
        outs = refs[22:]
    else:
        outs = refs[18:]
    r_o, lw_o, k_o, v_o, an_o, bn_o, gate_o, bonus_o, carry = outs
    tm = h_ref.shape[0]

    @pl.when(pl.program_id(0) % tiles_per_seq == 0)
    def _():
        carry[...] = jnp.zeros_like(carry)

    hn = _rms(h_ref[...], g_ref[2:3, :])
    first = lax.broadcasted_iota(jnp.int32, (tm, 1), 0) == 0
    prev = jnp.where(first, carry[0:1, :], pltpu.roll(hn, 1, axis=0))
    carry[0:1, :] = hn[tm - 1:tm, :]
    dx = prev - hn

    def lerp(i):
        return (hn + dx * mu_ref[i:i + 1, :]).astype(BF16)

    x_v = lerp(2)
    r = _dot(lerp(0), wr_ref[...])
    k = _dot(lerp(1), wk_ref[...])
    v = _dot(x_v, wv_ref[...])
    zw = w0_ref[...] + _dot(jnp.tanh(_dot(lerp(3), w1_ref[...])).astype(BF16), w2_ref[...])
    lw_o[...] = -DECAY_SCALE * jax.nn.sigmoid(zw)
    a = jax.nn.sigmoid(a0_ref[...] + _dot(_dot(lerp(4), a1_ref[...]).astype(BF16), a2_ref[...]))
    gate_o[...] = _dot(jax.nn.sigmoid(_dot(lerp(5), g1_ref[...])).astype(BF16), g2_ref[...]).astype(BF16)

    kk = k * kk_ref[...]
    ss = _head_sum(kk * kk, hsum_ref[...])
    kk = kk * lax.rsqrt(jnp.maximum(ss, KK_EPS_SQ))
    k = k * (1.0 + (a - 1.0) * ka_ref[...])
    if with_vres:
        mix = jax.nn.sigmoid(v0_ref[...] + _dot(_dot(x_v, v1_ref[...]).astype(BF16), v2_ref[...]))
        v = v + (vf_ref[...].astype(F32) - v) * mix
    r_o[...] = r.astype(BF16)
    k_o[...] = k.astype(BF16)
    v_o[...] = v.astype(BF16)
    an_o[...] = (-kk).astype(BF16)
    bn_o[...] = (kk * a).astype(BF16)
    bonus_o[...] = (_head_sum(r * k * rk_ref[...], hsum_ref[...]) * v).astype(BF16)


def _rwkv_in(h, g, mu, wr, wk, wv, w0, w1, w2, a0, a1, a2, g1, g2, k_k, k_a, r_k, hsum,
             vres, *, seq, tm=512):
    t, d = h.shape
    tm = min(tm, seq)
    row = lambda i: (i, 0)
    tile = pl.BlockSpec((tm, d), row)
    args = [h, g, mu, wr, wk, wv, w0, w1, w2, a0, a1, a2, g1, g2, k_k, k_a, r_k, hsum]
    in_specs = [tile] + [_resident(x.shape) for x in args[1:]]
    if vres is not None:
        v_first, v0, v1, v2 = vres
        args += [v_first, v0, v1, v2]
        in_specs += [tile, _resident(v0.shape), _resident(v1.shape), _resident(v2.shape)]
    kern = functools.partial(_rwkv_in_kernel, tiles_per_seq=seq // tm, with_vres=vres is not None)
    return pl.pallas_call(
        kern,
        grid=(t // tm,),
        in_specs=in_specs,
        out_specs=[tile] * 8,
        out_shape=[jax.ShapeDtypeStruct((t, d), F32 if i == 1 else BF16) for i in range(8)],
        scratch_shapes=[pltpu.VMEM((8, d), F32)],
        compiler_params=_params(),
        name="rwkv_in",
    )(*args)


def _wkv_kernel(r_ref, lw_ref, k_ref, v_ref, a_ref, b_ref, y_ref, s_scr, *, steps_per_seq):
    L = CHUNK
    rows, d = r_ref.shape
    n_chunks = rows // L

    @pl.when(pl.program_id(0) % steps_per_seq == 0)
    def _():
        s_scr[...] = jnp.zeros_like(s_scr)

    lw = lw_ref[...]
    ri = lax.broadcasted_iota(jnp.int32, (rows, rows), 0)
    cj = lax.broadcasted_iota(jnp.int32, (rows, rows), 1)
    tri = jnp.where((cj <= ri) & (cj // L == ri // L), 1.0, 0.0).astype(BF16)
    hi = lw.astype(BF16)
    rest = lw - hi.astype(F32)
    mid = rest.astype(BF16)
    low = (rest - mid.astype(F32)).astype(BF16)
    cum = _dot(tri, hi) + _dot(tri, mid) + _dot(tri, low)
    last = [cum[(c + 1) * L - 1:(c + 1) * L, :] for c in range(n_chunks)]
    cum_last = jnp.concatenate([jnp.broadcast_to(x, (L, d)) for x in last], axis=0)
    p_in = jnp.exp(cum)
    p_inv = jnp.exp(-cum)
    p_ex = jnp.exp(cum - lw)
    p_dec = jnp.exp(cum_last - cum)
    p_last = [jnp.exp(x) for x in last]

    r = r_ref[...].astype(F32)
    k = k_ref[...].astype(F32)
    a = a_ref[...].astype(F32)
    b = b_ref[...].astype(F32)
    rt = (r * p_in).astype(BF16)
    at = (a * p_ex).astype(BF16)
    kt = (k * p_inv).astype(BF16)
    bt = (b * p_inv).astype(BF16)
    kd = (k * p_dec).astype(BF16)
    bd = (b * p_dec).astype(BF16)
    vb = v_ref[...]

    head0 = lax.broadcasted_iota(jnp.int32, (1, LANES), 1) < HEAD_DIM
    ti = lax.broadcasted_iota(jnp.int32, (L, LANES), 0)
    sj = lax.broadcasted_iota(jnp.int32, (L, LANES), 1) % HEAD_DIM
    strict = sj < ti
    incl = sj <= ti
    eye = jnp.where(sj == ti, 1.0, 0.0)

    def stack(x):
        zero = jnp.zeros_like(x)
        return jnp.concatenate([jnp.where(head0, x, zero), jnp.where(head0, zero, x)], axis=0)

    n_pairs = d // LANES
    probs = [(slice(c * L, (c + 1) * L), slice(p * LANES, (p + 1) * LANES))
             for c in range(n_chunks) for p in range(n_pairs)]
    qs = range(len(probs))
    g = [_dot_nt(jnp.concatenate([at[q], rt[q]], axis=0),
                 jnp.concatenate([stack(bt[q]), stack(kt[q])], axis=0)) for q in probs]
    a_ab = [jnp.where(strict, x[:L, :LANES], 0.0) for x in g]
    a_ak = [jnp.where(strict, x[:L, LANES:], 0.0).astype(BF16) for x in g]
    a_rb = [jnp.where(incl, x[L:, :LANES], 0.0).astype(BF16) for x in g]
    a_rk = [jnp.where(incl, x[L:, LANES:], 0.0).astype(BF16) for x in g]
    v2 = [stack(vb[q]) for q in probs]
    avy = [_dot(jnp.concatenate([a_ak[q], a_rk[q]], axis=0), v2[q]) for q in qs]
    minv = [eye + x for x in a_ab]
    apow = [x.astype(BF16) for x in a_ab]
    apow = [_dot(x, stack(x)).astype(BF16) for x in apow]
    for i in range(4):
        res = [_dot(apow[q], jnp.concatenate([stack(minv[q].astype(BF16)), stack(apow[q])], axis=1))
               for q in qs]
        minv = [minv[q] + res[q][:, :LANES] for q in qs]
        apow = [x[:, LANES:].astype(BF16) for x in res]
    minv = [(minv[q] + _dot(apow[q], stack(minv[q].astype(BF16)))).astype(BF16) for q in qs]
    mw = [_dot(minv[q], jnp.concatenate([stack(at[probs[q]]), stack(avy[q][:L].astype(BF16))], axis=1))
          for q in qs]

    s = [s_scr[p] for p in range(n_pairs)]
    for c in range(n_chunks):
        cq = [c * n_pairs + p for p in range(n_pairs)]
        sb = [x.astype(BF16) for x in s]
        us = [_dot_nt(jnp.concatenate([mw[q][:, :LANES].astype(BF16), rt[probs[q]]], axis=0), sb[p])
              for p, q in enumerate(cq)]
        u2 = [stack((us[p][:L] + mw[q][:, LANES:]).astype(BF16)) for p, q in enumerate(cq)]
        for p, q in enumerate(cq):
            y_ref[probs[q]] = (us[p][L:] + _dot(a_rb[q], u2[p]) + avy[q][L:]).astype(BF16)
        upd = [_dot_tn(jnp.concatenate([u2[p], v2[q]], axis=0),
                       jnp.concatenate([stack(bd[probs[q]]), stack(kd[probs[q]])], axis=0))
               for p, q in enumerate(cq)]
        s = [s[p] * p_last[c][:, probs[q][1]] + upd[p] for p, q in enumerate(cq)]
    for p in range(n_pairs):
        s_scr[p] = s[p]


def _wkv(r, lw, k, v, an, bn, *, seq, n_chunks=4):
    t, d = r.shape
    rows = n_chunks * CHUNK
    row = lambda i: (i, 0)
    tile = pl.BlockSpec((rows, d), row)
    return pl.pallas_call(
        functools.partial(_wkv_kernel, steps_per_seq=seq // rows),
        grid=(t // rows,),
        in_specs=[tile] * 6,
        out_specs=tile,
        out_shape=jax.ShapeDtypeStruct((t, d), BF16),
        scratch_shapes=[pltpu.VMEM((d // LANES, LANES, LANES), F32)],
        compiler_params=_params(),
        name="wkv",
    )(r, lw, k, v, an, bn)


def _rwkv_out_kernel(h_ref, y_ref, bonus_ref, gate_ref, g_ref, lnx_ref, hsum_ref, wout_ref, o_ref):
    y = y_ref[...].astype(F32)
    yc = y - _head_sum(y, hsum_ref[...]) * (1.0 / HEAD_DIM)
    var = _head_sum(yc * yc, hsum_ref[...]) * (1.0 / HEAD_DIM)
    yn = yc * lax.rsqrt(var + GN_EPS) * lnx_ref[0:1, :] + lnx_ref[1:2, :]
    out = _dot(((yn + bonus_ref[...]) * gate_ref[...]).astype(BF16), wout_ref[...])
    o_ref[...] = h_ref[...] + _rms(out, g_ref[3:4, :])


def _rwkv_out(h, y, bonus, gate, g, lnx, hsum, w_out, *, tm=1024):
    t, d = h.shape
    tm = min(tm, t)
    tile = pl.BlockSpec((tm, d), lambda i: (i, 0))
    return pl.pallas_call(
        _rwkv_out_kernel,
        grid=(t // tm,),
        in_specs=[tile] * 4 + [_resident(g.shape), _resident(lnx.shape), _resident(hsum.shape),
                               _resident(w_out.shape)],
        out_specs=tile,
        out_shape=jax.ShapeDtypeStruct((t, d), F32),
        compiler_params=_params(),
        name="rwkv_out",
    )(h, y, bonus, gate, g, lnx, hsum, w_out)


def kernel(x, p, norm_g, ffn_w13, ffn_w2, ple_w_gate, ple_w_proj, a_w_in, a_b_in, a_v_gain, a_w_s, a_b_s, a_w_out, b_mu, b_w_in, b_w0, b_w1, b_w2, b_a0, b_a1, b_a2, b_g1, b_g2, b_k_k, b_k_a, b_r_k, b_lnx, b_w_out, b_v0, b_v1, b_v2):
    bsz, seq, d = x.shape
    depth = norm_g.shape[0]
    t = bsz * seq
    bf = lambda w: w.astype(BF16)
    row = lambda w: w.reshape(1, -1)

    head = jnp.arange(MXU_WIDTH) // HEAD_DIM
    hsum = (head[:, None] == head[None, :]).astype(BF16)

    h = x.reshape(t, d)
    p_flat = p.reshape(depth, t, -1)
    v_first = None
    for i in range(depth):
        g = norm_g[i]
        j = i // 2
        h = _ffn(h, g, ffn_w13, ffn_w2, i, 0, g_in=0, g_out=1)
        if i % 2 == 0:
            h = _gmlp(h, g, a_w_in, row(a_b_in[j]), row(a_v_gain[j]), a_w_s[j], a_b_s[j].T, a_w_out, j)
        else:
            vres = None
            if v_first is not None:
                vres = (v_first, row(b_v0[j - 1]), bf(b_v1[j - 1]), bf(b_v2[j - 1]))
            r, lw, k, v, an, bn, gate, bonus = _rwkv_in(
                h, g, b_mu[j], bf(b_w_in[j, 0]), bf(b_w_in[j, 1]), bf(b_w_in[j, 2]),
                row(b_w0[j]), bf(b_w1[j]), bf(b_w2[j]), row(b_a0[j]), bf(b_a1[j]), bf(b_a2[j]),
                bf(b_g1[j]), bf(b_g2[j]), row(b_k_k[j]), row(b_k_a[j]), row(b_r_k[j]), hsum,
                vres, seq=seq)
            if v_first is None:
                v_first = v
            y = _wkv(r, lw, k, v, an, bn, seq=seq)
            h = _rwkv_out(h, y, bonus, gate, g, b_lnx[j], hsum, bf(b_w_out[j]))
        h = _ffn(h, g, ffn_w13, ffn_w2, i, 1, g_in=4, g_out=5,
                 ple=(p_flat, bf(ple_w_gate[i]), bf(ple_w_proj[i])))
    return h.reshape(bsz, seq, d)
```

```python
import functools
import math

import jax
import jax.numpy as jnp
from jax import lax
from jax.experimental import pallas as pl
from jax.experimental.pallas import tpu as pltpu

F32 = jnp.float32
BF16 = jnp.bfloat16

RMS_EPS = 1e-6
LN_EPS = 1e-5
GN_EPS = 64e-5
KK_EPS_SQ = 1e-24
DECAY_SCALE = math.exp(-0.5)

HEAD_DIM = 64
GMLP_BLOCK = 128
GMLP_GROUPS = 8
CHUNK = 64
LANES = 128
MXU_WIDTH = 256

V7X_VMEM_LIMIT_BYTES = 56 * 1024 * 1024


def _dot(a, b):
    return jnp.dot(a, b, preferred_element_type=F32)


def _dot_nt(a, b):
    return lax.dot_general(a, b, (((1,), (1,)), ((), ())), preferred_element_type=F32)


def _dot_tn(a, b):
    return lax.dot_general(a, b, (((0,), (0,)), ((), ())), preferred_element_type=F32)


def _rms(x, g):
    return x * lax.rsqrt(jnp.mean(x * x, axis=-1, keepdims=True) + RMS_EPS) * g


def _head_sum(x, hsum):
    w = hsum.shape[0]
    xb = x.astype(BF16)
    return jnp.concatenate([_dot(xb[:, i:i + w], hsum) for i in range(0, x.shape[1], w)], axis=1)


def _gelu_tanh(x):
    c = -2.0 * math.sqrt(2.0 / math.pi) * math.log2(math.e)
    z = x * (x * x * (c * 0.044715) + c)
    return x / (1.0 + jnp.exp2(z))


def _resident(shape):
    nd = len(shape)
    return pl.BlockSpec(shape, lambda *_: (0,) * nd, pipeline_mode=pl.Buffered(1))


def _layer_block(arr, *idx):
    rest = arr.shape[len(idx):]
    return pl.BlockSpec((None,) * len(idx) + rest, lambda *_: idx + (0,) * len(rest),
                        pipeline_mode=pl.Buffered(1))


def _params(n_axes=1):
    return pltpu.CompilerParams(
        dimension_semantics=("arbitrary",) * n_axes,
        vmem_limit_bytes=V7X_VMEM_LIMIT_BYTES)


def _ffn_kernel(*refs, d_ff, f_chunk, g_in, g_out, with_ple):
    if with_ple:
        h_ref, g_ref, w13_ref, w2_ref, p_ref, wg_ref, wp_ref, o_ref, acc_ref = refs
    else:
        h_ref, g_ref, w13_ref, w2_ref, o_ref, acc_ref = refs
    h = h_ref[...]
    xn = _rms(h, g_ref[g_in:g_in + 1, :]).astype(BF16)
    for j in range(d_ff // f_chunk):
        lo = j * f_chunk
        gate = _dot(xn, w13_ref[:, lo:lo + f_chunk].astype(BF16))
        up = _dot(xn, w13_ref[:, d_ff + lo:d_ff + lo + f_chunk].astype(BF16))
        act = (gate * jax.nn.sigmoid(gate) * up).astype(BF16)
        part = _dot(act, w2_ref[lo:lo + f_chunk, :].astype(BF16))
        if j == 0:
            acc_ref[...] = part
        else:
            acc_ref[...] += part
    h = h + 0.5 * _rms(acc_ref[...], g_ref[g_out:g_out + 1, :])
    if with_ple:
        hn = _rms(h, g_ref[6:7, :]).astype(BF16)
        gate = jax.nn.sigmoid(_dot(hn, wg_ref[...]))
        proj = _dot(p_ref[...].astype(BF16), wp_ref[...])
        h = h + _rms(gate * proj, g_ref[7:8, :])
    o_ref[...] = h


def _ffn(h, g, w13_all, w2_all, layer, half, *, g_in, g_out, ple=None, tm=512, f_chunk=256):
    t, d = h.shape
    d_ff = w2_all.shape[2]
    tm = min(tm, t)
    row = lambda i: (i, 0)
    in_specs = [pl.BlockSpec((tm, d), row), _resident(g.shape),
                _layer_block(w13_all, layer, half), _layer_block(w2_all, layer, half)]
    args = [h, g, w13_all, w2_all]
    if ple is not None:
        p_all, wg, wp = ple
        in_specs += [pl.BlockSpec((None, tm, p_all.shape[2]), lambda i: (layer, i, 0)),
                     _resident(wg.shape), _resident(wp.shape)]
        args += [p_all, wg, wp]
    kern = functools.partial(_ffn_kernel, d_ff=d_ff, f_chunk=f_chunk, g_in=g_in, g_out=g_out,
                             with_ple=ple is not None)
    return pl.pallas_call(
        kern,
        grid=(t // tm,),
        in_specs=in_specs,
        out_specs=pl.BlockSpec((tm, d), row),
        out_shape=jax.ShapeDtypeStruct((t, d), F32),
        scratch_shapes=[pltpu.VMEM((tm, d), F32)],
        compiler_params=_params(),
        name="ffn_ple" if ple is not None else "ffn",
    )(*args)


def _gmlp_kernel(h_ref, g_ref, win_ref, bin_ref, vgain_ref, ws_ref, bst_ref, wout_ref,
                 o_ref, v_scr, y_scr, *, width):
    tm = h_ref.shape[0]
    gw = width // GMLP_GROUPS
    h = h_ref[...]
    hn = _rms(h, g_ref[2:3, :]).astype(BF16)
    for g in range(GMLP_GROUPS):
        lo = width + g * gw
        v_scr[:, g * gw:(g + 1) * gw] = _gelu_tanh(
            _dot(hn, win_ref[:, lo:lo + gw].astype(BF16)) + bin_ref[:, lo:lo + gw])
    v = v_scr[...]
    mean = jnp.mean(v, axis=-1, keepdims=True)
    vc = v - mean
    var = jnp.mean(vc * vc, axis=-1, keepdims=True)
    v_scr[...] = vc * lax.rsqrt(var + LN_EPS) * vgain_ref[...]
    ci = lax.broadcasted_iota(jnp.int32, (GMLP_BLOCK, GMLP_BLOCK), 0) // CHUNK
    cj = lax.broadcasted_iota(jnp.int32, (GMLP_BLOCK, GMLP_BLOCK), 1) // CHUNK
    causal = cj <= ci
    for g in range(GMLP_GROUPS):
        ws = jnp.where(causal, ws_ref[g], 0.0).astype(BF16)
        bias = bst_ref[:, g:g + 1]
        u = _gelu_tanh(
            _dot(hn, win_ref[:, g * gw:(g + 1) * gw].astype(BF16)) + bin_ref[:, g * gw:(g + 1) * gw])
        for c in range(tm // GMLP_BLOCK):
            rows = slice(c * GMLP_BLOCK, (c + 1) * GMLP_BLOCK)
            s = _dot(ws, v_scr[rows, g * gw:(g + 1) * gw].astype(BF16)) + bias
            y_scr[rows, g * gw:(g + 1) * gw] = (u[rows, :] * s).astype(BF16)
    m = _dot(y_scr[...], wout_ref[...].astype(BF16))
    o_ref[...] = h + _rms(m, g_ref[3:4, :])


def _gmlp(h, g, w_in_all, b_in, v_gain, w_s, b_s_t, w_out_all, layer, *, tm=512):
    t, d = h.shape
    width = w_out_all.shape[1]
    tm = min(tm, t)
    row = lambda i: (i, 0)
    return pl.pallas_call(
        functools.partial(_gmlp_kernel, width=width),
        grid=(t // tm,),
        in_specs=[pl.BlockSpec((tm, d), row), _resident(g.shape), _layer_block(w_in_all, layer),
                  _resident(b_in.shape), _resident(v_gain.shape), _resident(w_s.shape),
                  _resident(b_s_t.shape), _layer_block(w_out_all, layer)],
        out_specs=pl.BlockSpec((tm, d), row),
        out_shape=jax.ShapeDtypeStruct((t, d), F32),
        scratch_shapes=[pltpu.VMEM((tm, width), F32), pltpu.VMEM((tm, width), BF16)],
        compiler_params=_params(),
        name="gmlp",
    )(h, g, w_in_all, b_in, v_gain, w_s, b_s_t, w_out_all)


def _rwkv_in_kernel(*refs, tiles_per_seq, with_vres):
    (h_ref, g_ref, mu_ref, wr_ref, wk_ref, wv_ref, w0_ref, w1_ref, w2_ref, a0_ref, a1_ref, a2_ref,
     g1_ref, g2_ref, kk_ref, ka_ref, rk_ref, hsum_ref) = refs[:18]
    if with_vres:
        vf_ref, v0_ref, v1_ref, v2_ref = refs[18:22]
        outs = refs[22:]
    else:
        outs = refs[18:]
    r_o, lw_o, k_o, v_o, an_o, bn_o, gate_o, bonus_o, carry = outs
    tm = h_ref.shape[0]

    @pl.when(pl.program_id(0) % tiles_per_seq == 0)
    def _():
        carry[...] = jnp.zeros_like(carry)

    hn = _rms(h_ref[...], g_ref[2:3, :])
    first = lax.broadcasted_iota(jnp.int32, (tm, 1), 0) == 0
    prev = jnp.where(first, carry[0:1, :], pltpu.roll(hn, 1, axis=0))
    carry[0:1, :] = hn[tm - 1:tm, :]
    dx = prev - hn

    def lerp(i):
        return (hn + dx * mu_ref[i:i + 1, :]).astype(BF16)

    x_v = lerp(2)
    r = _dot(lerp(0), wr_ref[...])
    k = _dot(lerp(1), wk_ref[...])
    v = _dot(x_v, wv_ref[...])
    zw = w0_ref[...] + _dot(jnp.tanh(_dot(lerp(3), w1_ref[...])).astype(BF16), w2_ref[...])
    lw_o[...] = -DECAY_SCALE * jax.nn.sigmoid(zw)
    a = jax.nn.sigmoid(a0_ref[...] + _dot(_dot(lerp(4), a1_ref[...]).astype(BF16), a2_ref[...]))
    gate_o[...] = _dot(jax.nn.sigmoid(_dot(lerp(5), g1_ref[...])).astype(BF16), g2_ref[...]).astype(BF16)

    kk = k * kk_ref[...]
    ss = _head_sum(kk * kk, hsum_ref[...])
    kk = kk * lax.rsqrt(jnp.maximum(ss, KK_EPS_SQ))
    k = k * (1.0 + (a - 1.0) * ka_ref[...])
    if with_vres:
        mix = jax.nn.sigmoid(v0_ref[...] + _dot(_dot(x_v, v1_ref[...]).astype(BF16), v2_ref[...]))
        v = v + (vf_ref[...].astype(F32) - v) * mix
    r_o[...] = r.astype(BF16)
    k_o[...] = k.astype(BF16)
    v_o[...] = v.astype(BF16)
    an_o[...] = (-kk).astype(BF16)
    bn_o[...] = (kk * a).astype(BF16)
    bonus_o[...] = (_head_sum(r * k * rk_ref[...], hsum_ref[...]) * v).astype(BF16)


def _rwkv_in(h, g, mu, wr, wk, wv, w0, w1, w2, a0, a1, a2, g1, g2, k_k, k_a, r_k, hsum,
             vres, *, seq, tm=512):
    t, d = h.shape
    tm = min(tm, seq)
    row = lambda i: (i, 0)
    tile = pl.BlockSpec((tm, d), row)
    args = [h, g, mu, wr, wk, wv, w0, w1, w2, a0, a1, a2, g1, g2, k_k, k_a, r_k, hsum]
    in_specs = [tile] + [_resident(x.shape) for x in args[1:]]
    if vres is not None:
        v_first, v0, v1, v2 = vres
        args += [v_first, v0, v1, v2]
        in_specs += [tile, _resident(v0.shape), _resident(v1.shape), _resident(v2.shape)]
    kern = functools.partial(_rwkv_in_kernel, tiles_per_seq=seq // tm, with_vres=vres is not None)
    return pl.pallas_call(
        kern,
        grid=(t // tm,),
        in_specs=in_specs,
        out_specs=[tile] * 8,
        out_shape=[jax.ShapeDtypeStruct((t, d), F32 if i == 1 else BF16) for i in range(8)],
        scratch_shapes=[pltpu.VMEM((8, d), F32)],
        compiler_params=_params(),
        name="rwkv_in",
    )(*args)


def _wkv_kernel(r_ref, lw_ref, k_ref, v_ref, a_ref, b_ref, h_ref, bonus_ref, gate_ref, g_ref, lnx_ref,
                hsum_ref, wout_ref, o_ref, s_scr, y_scr):
    L = CHUNK
    n_seq, seq_rows, d = r_ref.shape
    rows = n_seq * seq_rows
    n_chunks = seq_rows // L
    flat = lambda ref: ref[...].reshape(rows, d)

    @pl.when(pl.program_id(1) == 0)
    def _():
        s_scr[...] = jnp.zeros_like(s_scr)

    lw = flat(lw_ref)
    ri = lax.broadcasted_iota(jnp.int32, (rows, rows), 0)
    cj = lax.broadcasted_iota(jnp.int32, (rows, rows), 1)
    tri = jnp.where((cj <= ri) & (cj // L == ri // L), 1.0, 0.0).astype(BF16)
    hi = lw.astype(BF16)
    rest = lw - hi.astype(F32)
    mid = rest.astype(BF16)
    low = (rest - mid.astype(F32)).astype(BF16)
    cum = _dot(tri, hi) + _dot(tri, mid) + _dot(tri, low)
    last = [cum[(c + 1) * L - 1:(c + 1) * L, :] for c in range(rows // L)]
    p_in = jnp.exp(cum)
    p_inv = jnp.exp(-cum)
    p_ex = jnp.exp(cum - lw)
    p_last = [jnp.exp(x) for x in last]

    r = flat(r_ref).astype(F32)
    k = flat(k_ref).astype(F32)
    a = flat(a_ref).astype(F32)
    b = flat(b_ref).astype(F32)
    rt = (r * p_in).astype(BF16)
    at = (a * p_ex).astype(BF16)
    kt = (k * p_inv).astype(BF16)
    bt = (b * p_inv).astype(BF16)
    vb = flat(v_ref)

    head0 = lax.broadcasted_iota(jnp.int32, (1, LANES), 1) < HEAD_DIM
    ti = lax.broadcasted_iota(jnp.int32, (L, LANES), 0)
    sj = lax.broadcasted_iota(jnp.int32, (L, LANES), 1) % HEAD_DIM
    strict = sj < ti
    incl = sj <= ti
    eye = jnp.where(sj == ti, 1.0, 0.0)

    def stack(x):
        zero = jnp.zeros_like(x)
        return jnp.concatenate([jnp.where(head0, x, zero), jnp.where(head0, zero, x)], axis=0)

    n_pairs = d // LANES
    probs = [(slice(c * L, (c + 1) * L), slice(p * LANES, (p + 1) * LANES))
             for c in range(rows // L) for p in range(n_pairs)]
    qs = range(len(probs))
    g = [_dot_nt(jnp.concatenate([at[q], rt[q]], axis=0),
                 jnp.concatenate([stack(bt[q]), stack(kt[q])], axis=0)) for q in probs]
    a_ab = [jnp.where(strict, x[:L, :LANES], 0.0) for x in g]
    a_ak = [jnp.where(strict, x[:L, LANES:], 0.0).astype(BF16) for x in g]
    a_rb = [jnp.where(incl, x[L:, :LANES], 0.0).astype(BF16) for x in g]
    a_rk = [jnp.where(incl, x[L:, LANES:], 0.0).astype(BF16) for x in g]
    v2 = [stack(vb[q]) for q in probs]
    avy = [_dot(jnp.concatenate([a_ak[q], a_rk[q]], axis=0), v2[q]) for q in qs]
    minv = [eye + x for x in a_ab]
    apow = [x.astype(BF16) for x in a_ab]
    apow = [_dot(x, stack(x)).astype(BF16) for x in apow]
    for i in range(4):
        res = [_dot(apow[q], jnp.concatenate([stack(minv[q].astype(BF16)), stack(apow[q])], axis=1))
               for q in qs]
        minv = [minv[q] + res[q][:, :LANES] for q in qs]
        apow = [x[:, LANES:].astype(BF16) for x in res]
    minv = [(minv[q] + _dot(apow[q], stack(minv[q].astype(BF16)))).astype(BF16) for q in qs]
    mw = [_dot(minv[q], jnp.concatenate([stack(at[probs[q]]), stack(avy[q][:L].astype(BF16))], axis=1))
          for q in qs]

    n_states = n_seq * n_pairs
    s = [s_scr[i] for i in range(n_states)]
    for c in range(n_chunks):
        cq = [(b * n_chunks + c) * n_pairs + p for b in range(n_seq) for p in range(n_pairs)]
        sb = [x.astype(BF16) for x in s]
        us = [_dot_nt(jnp.concatenate([mw[q][:, :LANES].astype(BF16), rt[probs[q]]], axis=0), sb[i])
              for i, q in enumerate(cq)]
        u2 = [stack((us[i][:L] + mw[q][:, LANES:]).astype(BF16)) for i, q in enumerate(cq)]
        for i, q in enumerate(cq):
            y_scr[probs[q]] = us[i][L:] + _dot(a_rb[q], u2[i]) + avy[q][L:]
        upd = [_dot_tn(jnp.concatenate([u2[i], v2[q]], axis=0),
                       jnp.concatenate([stack(bt[probs[q]]), stack(kt[probs[q]])], axis=0))
               for i, q in enumerate(cq)]
        s = [(s[i] + upd[i]) * p_last[q // n_pairs][:, probs[q][1]] for i, q in enumerate(cq)]
    for i in range(n_states):
        s_scr[i] = s[i]

    y = y_scr[...]
    yc = y - _head_sum(y, hsum_ref[...]) * (1.0 / HEAD_DIM)
    var = _head_sum(yc * yc, hsum_ref[...]) * (1.0 / HEAD_DIM)
    yn = yc * lax.rsqrt(var + GN_EPS) * lnx_ref[0:1, :] + lnx_ref[1:2, :]
    out = _dot(((yn + flat(bonus_ref)) * flat(gate_ref)).astype(BF16), wout_ref[...])
    o_ref[...] = (flat(h_ref) + _rms(out, g_ref[3:4, :])).reshape(n_seq, seq_rows, d)


def _wkv(r, lw, k, v, an, bn, h, bonus, gate, g, lnx, hsum, w_out, *, seq, n_seq=2, n_chunks=2):
    t, d = r.shape
    bsz = t // seq
    n_seq = min(n_seq, bsz)
    seq_rows = n_chunks * CHUNK
    tile = pl.BlockSpec((n_seq, seq_rows, d), lambda i, j: (i, j, 0))
    by_seq = lambda x: x.reshape(bsz, seq, d)
    out = pl.pallas_call(
        _wkv_kernel,
        grid=(bsz // n_seq, seq // seq_rows),
        in_specs=[tile] * 9 + [_resident(g.shape), _resident(lnx.shape), _resident(hsum.shape),
                               _resident(w_out.shape)],
        out_specs=tile,
        out_shape=jax.ShapeDtypeStruct((bsz, seq, d), F32),
        scratch_shapes=[pltpu.VMEM((n_seq * (d // LANES), LANES, LANES), F32),
                        pltpu.VMEM((n_seq * seq_rows, d), F32)],
        compiler_params=_params(2),
        name="wkv",
    )(*(by_seq(x) for x in (r, lw, k, v, an, bn, h, bonus, gate)), g, lnx, hsum, w_out)
    return out.reshape(t, d)


def kernel(x, p, norm_g, ffn_w13, ffn_w2, ple_w_gate, ple_w_proj, a_w_in, a_b_in, a_v_gain, a_w_s, a_b_s, a_w_out, b_mu, b_w_in, b_w0, b_w1, b_w2, b_a0, b_a1, b_a2, b_g1, b_g2, b_k_k, b_k_a, b_r_k, b_lnx, b_w_out, b_v0, b_v1, b_v2):
    bsz, seq, d = x.shape
    depth = norm_g.shape[0]
    t = bsz * seq
    bf = lambda w: w.astype(BF16)
    row = lambda w: w.reshape(1, -1)

    head = jnp.arange(MXU_WIDTH) // HEAD_DIM
    hsum = (head[:, None] == head[None, :]).astype(BF16)

    h = x.reshape(t, d)
    p_flat = p.reshape(depth, t, -1)
    v_first = None
    for i in range(depth):
        g = norm_g[i]
        j = i // 2
        h = _ffn(h, g, ffn_w13, ffn_w2, i, 0, g_in=0, g_out=1)
        if i % 2 == 0:
            h = _gmlp(h, g, a_w_in, row(a_b_in[j]), row(a_v_gain[j]), a_w_s[j], a_b_s[j].T, a_w_out, j)
        else:
            vres = None
            if v_first is not None:
                vres = (v_first, row(b_v0[j - 1]), bf(b_v1[j - 1]), bf(b_v2[j - 1]))
            r, lw, k, v, an, bn, gate, bonus = _rwkv_in(
                h, g, b_mu[j], bf(b_w_in[j, 0]), bf(b_w_in[j, 1]), bf(b_w_in[j, 2]),
                row(b_w0[j]), bf(b_w1[j]), bf(b_w2[j]), row(b_a0[j]), bf(b_a1[j]), bf(b_a2[j]),
                bf(b_g1[j]), bf(b_g2[j]), row(b_k_k[j]), row(b_k_a[j]), row(b_r_k[j]), hsum,
                vres, seq=seq)
            if v_first is None:
                v_first = v
            h = _wkv(r, lw, k, v, an, bn, h, bonus, gate, g, b_lnx[j], hsum, bf(b_w_out[j]), seq=seq)
        h = _ffn(h, g, ffn_w13, ffn_w2, i, 1, g_in=4, g_out=5,
                 ple=(p_flat, bf(ple_w_gate[i]), bf(ple_w_proj[i])))
    return h.reshape(bsz, seq, d)
```

```python
import functools
import math

import jax
import jax.numpy as jnp
from jax import lax
from jax.experimental import pallas as pl
from jax.experimental.pallas import tpu as pltpu

F32 = jnp.float32
BF16 = jnp.bfloat16

RMS_EPS = 1e-6
LN_EPS = 1e-5
GN_EPS = 64e-5
KK_EPS_SQ = 1e-24
DECAY_SCALE = math.exp(-0.5)

HEAD_DIM = 64
GMLP_BLOCK = 128
GMLP_GROUPS = 8
CHUNK = 64
LANES = 128
MXU_WIDTH = 256

V7X_VMEM_LIMIT_BYTES = 56 * 1024 * 1024


def _dot(a, b):
    return jnp.dot(a, b, preferred_element_type=F32)


def _dot_nt(a, b):
    return lax.dot_general(a, b, (((1,), (1,)), ((), ())), preferred_element_type=F32)


def _dot_tn(a, b):
    return lax.dot_general(a, b, (((0,), (0,)), ((), ())), preferred_element_type=F32)


def _rms(x, g):
    return x * lax.rsqrt(jnp.mean(x * x, axis=-1, keepdims=True) + RMS_EPS) * g


def _head_sum(x, hsum):
    w = hsum.shape[0]
    xb = x.astype(BF16)
    return jnp.concatenate([_dot(xb[:, i:i + w], hsum) for i in range(0, x.shape[1], w)], axis=1)


def _gelu_tanh(x):
    c = -2.0 * math.sqrt(2.0 / math.pi) * math.log2(math.e)
    z = x * (x * x * (c * 0.044715) + c)
    return x / (1.0 + jnp.exp2(z))


def _resident(shape):
    nd = len(shape)
    return pl.BlockSpec(shape, lambda *_: (0,) * nd, pipeline_mode=pl.Buffered(1))


def _layer_block(arr, *idx):
    rest = arr.shape[len(idx):]
    return pl.BlockSpec((None,) * len(idx) + rest, lambda *_: idx + (0,) * len(rest),
                        pipeline_mode=pl.Buffered(1))


def _params(n_axes=1):
    return pltpu.CompilerParams(
        dimension_semantics=("arbitrary",) * n_axes,
        vmem_limit_bytes=V7X_VMEM_LIMIT_BYTES)


def _ffn_kernel(*refs, d_ff, f_chunk, g_in, g_out, with_ple):
    if with_ple:
        h_ref, g_ref, w13_ref, w2_ref, p_ref, wg_ref, wp_ref, o_ref, acc_ref = refs
    else:
        h_ref, g_ref, w13_ref, w2_ref, o_ref, acc_ref = refs
    h = h_ref[...]
    xn = _rms(h, g_ref[g_in:g_in + 1, :]).astype(BF16)
    for j in range(d_ff // f_chunk):
        lo = j * f_chunk
        gate = _dot(xn, w13_ref[:, lo:lo + f_chunk].astype(BF16))
        up = _dot(xn, w13_ref[:, d_ff + lo:d_ff + lo + f_chunk].astype(BF16))
        act = (gate * jax.nn.sigmoid(gate) * up).astype(BF16)
        part = _dot(act, w2_ref[lo:lo + f_chunk, :].astype(BF16))
        if j == 0:
            acc_ref[...] = part
        else:
            acc_ref[...] += part
    h = h + 0.5 * _rms(acc_ref[...], g_ref[g_out:g_out + 1, :])
    if with_ple:
        hn = _rms(h, g_ref[6:7, :]).astype(BF16)
        gate = jax.nn.sigmoid(_dot(hn, wg_ref[...]))
        proj = _dot(p_ref[...].astype(BF16), wp_ref[...])
        h = h + _rms(gate * proj, g_ref[7:8, :])
    o_ref[...] = h


def _ffn(h, g, w13_all, w2_all, layer, half, *, g_in, g_out, ple=None, tm=512, f_chunk=256):
    t, d = h.shape
    d_ff = w2_all.shape[2]
    tm = min(tm, t)
    row = lambda i: (i, 0)
    in_specs = [pl.BlockSpec((tm, d), row), _resident(g.shape),
                _layer_block(w13_all, layer, half), _layer_block(w2_all, layer, half)]
    args = [h, g, w13_all, w2_all]
    if ple is not None:
        p_all, wg, wp = ple
        in_specs += [pl.BlockSpec((None, tm, p_all.shape[2]), lambda i: (layer, i, 0)),
                     _resident(wg.shape), _resident(wp.shape)]
        args += [p_all, wg, wp]
    kern = functools.partial(_ffn_kernel, d_ff=d_ff, f_chunk=f_chunk, g_in=g_in, g_out=g_out,
                             with_ple=ple is not None)
    return pl.pallas_call(
        kern,
        grid=(t // tm,),
        in_specs=in_specs,
        out_specs=pl.BlockSpec((tm, d), row),
        out_shape=jax.ShapeDtypeStruct((t, d), F32),
        scratch_shapes=[pltpu.VMEM((tm, d), F32)],
        compiler_params=_params(),
        name="ffn_ple" if ple is not None else "ffn",
    )(*args)


def _gmlp_kernel(h_ref, g_ref, win_ref, bin_ref, vgain_ref, ws_ref, bst_ref, wout_ref,
                 o_ref, v_scr, y_scr, *, width):
    tm = h_ref.shape[0]
    gw = width // GMLP_GROUPS
    h = h_ref[...]
    hn = _rms(h, g_ref[2:3, :]).astype(BF16)
    for g in range(GMLP_GROUPS):
        lo = width + g * gw
        v_scr[:, g * gw:(g + 1) * gw] = _gelu_tanh(
            _dot(hn, win_ref[:, lo:lo + gw].astype(BF16)) + bin_ref[:, lo:lo + gw])
    v = v_scr[...]
    mean = jnp.mean(v, axis=-1, keepdims=True)
    vc = v - mean
    var = jnp.mean(vc * vc, axis=-1, keepdims=True)
    y_scr[...] = (vc * lax.rsqrt(var + LN_EPS) * vgain_ref[...]).astype(BF16)
    ci = lax.broadcasted_iota(jnp.int32, (GMLP_BLOCK, GMLP_BLOCK), 0) // CHUNK
    cj = lax.broadcasted_iota(jnp.int32, (GMLP_BLOCK, GMLP_BLOCK), 1) // CHUNK
    causal = cj <= ci
    for g in range(GMLP_GROUPS):
        ws = jnp.where(causal, ws_ref[g], 0.0).astype(BF16)
        bias = bst_ref[:, g:g + 1]
        u = _gelu_tanh(
            _dot(hn, win_ref[:, g * gw:(g + 1) * gw].astype(BF16)) + bin_ref[:, g * gw:(g + 1) * gw])
        for c in range(tm // GMLP_BLOCK):
            rows = slice(c * GMLP_BLOCK, (c + 1) * GMLP_BLOCK)
            cols = slice(g * gw, (g + 1) * gw)
            s = _dot(ws, y_scr[rows, cols]) + bias
            y_scr[rows, cols] = (u[rows, :] * s).astype(BF16)
    m = _dot(y_scr[...], wout_ref[...].astype(BF16))
    o_ref[...] = h + _rms(m, g_ref[3:4, :])


def _gmlp(h, g, w_in_all, b_in, v_gain, w_s, b_s_t, w_out_all, layer, *, tm=512):
    t, d = h.shape
    width = w_out_all.shape[1]
    tm = min(tm, t)
    row = lambda i: (i, 0)
    return pl.pallas_call(
        functools.partial(_gmlp_kernel, width=width),
        grid=(t // tm,),
        in_specs=[pl.BlockSpec((tm, d), row), _resident(g.shape), _layer_block(w_in_all, layer),
                  _resident(b_in.shape), _resident(v_gain.shape), _resident(w_s.shape),
                  _resident(b_s_t.shape), _layer_block(w_out_all, layer)],
        out_specs=pl.BlockSpec((tm, d), row),
        out_shape=jax.ShapeDtypeStruct((t, d), F32),
        scratch_shapes=[pltpu.VMEM((tm, width), F32), pltpu.VMEM((tm, width), BF16)],
        compiler_params=_params(),
        name="gmlp",
    )(h, g, w_in_all, b_in, v_gain, w_s, b_s_t, w_out_all)


def _rwkv_in_kernel(*refs, tiles_per_seq, with_vres):
    (h_ref, g_ref, mu_ref, wr_ref, wk_ref, wv_ref, w0_ref, w1_ref, w2_ref, a0_ref, a1_ref, a2_ref,
     g1_ref, g2_ref, kk_ref, ka_ref, rk_ref, hsum_ref) = refs[:18]
    if with_vres:
        vf_ref, v0_ref, v1_ref, v2_ref = refs[18:22]
        outs = refs[22:]
    else:
        outs = refs[18:]
    rt_o, at_o, kt_o, bt_o, v_o, plast_o, gate_o, bonus_o, carry = outs
    tm = h_ref.shape[0]
    L = CHUNK

    @pl.when(pl.program_id(0) % tiles_per_seq == 0)
    def _():
        carry[...] = jnp.zeros_like(carry)

    hn = _rms(h_ref[...], g_ref[2:3, :])
    first = lax.broadcasted_iota(jnp.int32, (tm, 1), 0) == 0
    prev = jnp.where(first, carry[0:1, :], pltpu.roll(hn, 1, axis=0))
    carry[0:1, :] = hn[tm - 1:tm, :]
    dx = prev - hn

    def lerp(i):
        return (hn + dx * mu_ref[i:i + 1, :]).astype(BF16)

    x_v = lerp(2)
    r = _dot(lerp(0), wr_ref[...])
    k = _dot(lerp(1), wk_ref[...])
    v = _dot(x_v, wv_ref[...])
    zw = w0_ref[...] + _dot(jnp.tanh(_dot(lerp(3), w1_ref[...])).astype(BF16), w2_ref[...])
    lw = -DECAY_SCALE * jax.nn.sigmoid(zw)
    a = jax.nn.sigmoid(a0_ref[...] + _dot(_dot(lerp(4), a1_ref[...]).astype(BF16), a2_ref[...]))
    gate_o[...] = _dot(jax.nn.sigmoid(_dot(lerp(5), g1_ref[...])).astype(BF16), g2_ref[...]).astype(BF16)

    kk = k * kk_ref[...]
    ss = _head_sum(kk * kk, hsum_ref[...])
    kk = kk * lax.rsqrt(jnp.maximum(ss, KK_EPS_SQ))
    k = k * (1.0 + (a - 1.0) * ka_ref[...])
    if with_vres:
        mix = jax.nn.sigmoid(v0_ref[...] + _dot(_dot(x_v, v1_ref[...]).astype(BF16), v2_ref[...]))
        v = v + (vf_ref[...].astype(F32) - v) * mix
    v_o[...] = v.astype(BF16)
    bonus_o[...] = (_head_sum(r * k * rk_ref[...], hsum_ref[...]) * v).astype(BF16)

    ri = lax.broadcasted_iota(jnp.int32, (L, L), 0)
    cj = lax.broadcasted_iota(jnp.int32, (L, L), 1)
    tri = jnp.where(cj <= ri, 1.0, 0.0).astype(BF16)
    hi = lw.astype(BF16)
    lo = (lw - hi.astype(F32)).astype(BF16)
    cums = []
    for c in range(tm // L):
        rows = slice(c * L, (c + 1) * L)
        cum_c = _dot(tri, hi[rows, :]) + _dot(tri, lo[rows, :])
        plast_o[c] = jnp.exp(cum_c[L - 1:L, :])
        cums.append(cum_c)
    cum = jnp.concatenate(cums, axis=0)
    p_inv = jnp.exp(-cum)
    rt_o[...] = (r * jnp.exp(cum)).astype(BF16)
    at_o[...] = (-kk * jnp.exp(cum - lw)).astype(BF16)
    kt_o[...] = (k * p_inv).astype(BF16)
    bt_o[...] = (kk * a * p_inv).astype(BF16)


def _rwkv_in(h, g, mu, wr, wk, wv, w0, w1, w2, a0, a1, a2, g1, g2, k_k, k_a, r_k, hsum,
             vres, *, seq, tm=512):
    t, d = h.shape
    tm = min(tm, seq)
    row = lambda i: (i, 0)
    tile = pl.BlockSpec((tm, d), row)
    args = [h, g, mu, wr, wk, wv, w0, w1, w2, a0, a1, a2, g1, g2, k_k, k_a, r_k, hsum]
    in_specs = [tile] + [_resident(x.shape) for x in args[1:]]
    if vres is not None:
        v_first, v0, v1, v2 = vres
        args += [v_first, v0, v1, v2]
        in_specs += [tile, _resident(v0.shape), _resident(v1.shape), _resident(v2.shape)]
    kern = functools.partial(_rwkv_in_kernel, tiles_per_seq=seq // tm, with_vres=vres is not None)
    return pl.pallas_call(
        kern,
        grid=(t // tm,),
        in_specs=in_specs,
        out_specs=[tile] * 5 + [pl.BlockSpec((tm // CHUNK, 1, d), lambda i: (i, 0, 0))] + [tile] * 2,
        out_shape=[jax.ShapeDtypeStruct((t, d), BF16)] * 5
        + [jax.ShapeDtypeStruct((t // CHUNK, 1, d), F32)] + [jax.ShapeDtypeStruct((t, d), BF16)] * 2,
        scratch_shapes=[pltpu.VMEM((8, d), F32)],
        compiler_params=_params(),
        name="rwkv_in",
    )(*args)


def _wkv_kernel(rt_ref, at_ref, kt_ref, bt_ref, v_ref, plast_ref, h_ref, bonus_ref, gate_ref, g_ref, lnx_ref,
                hsum_ref, wout_ref, o_ref, s_scr, y_scr):
    L = CHUNK
    n_seq, seq_rows, d = rt_ref.shape
    rows = n_seq * seq_rows
    n_chunks = seq_rows // L
    flat = lambda ref: ref[...].reshape(rows, d)

    @pl.when(pl.program_id(1) == 0)
    def _():
        s_scr[...] = jnp.zeros_like(s_scr)

    rt, at, kt, bt, vb = flat(rt_ref), flat(at_ref), flat(kt_ref), flat(bt_ref), flat(v_ref)
    p_last = [plast_ref[b, c] for b in range(n_seq) for c in range(n_chunks)]

    head0 = lax.broadcasted_iota(jnp.int32, (1, LANES), 1) < HEAD_DIM
    ti = lax.broadcasted_iota(jnp.int32, (L, LANES), 0)
    sj = lax.broadcasted_iota(jnp.int32, (L, LANES), 1) % HEAD_DIM
    strict = sj < ti
    incl = sj <= ti
    eye = jnp.where(sj == ti, 1.0, 0.0)

    def stack(x):
        zero = jnp.zeros_like(x)
        return jnp.concatenate([jnp.where(head0, x, zero), jnp.where(head0, zero, x)], axis=0)

    n_pairs = d // LANES
    probs = [(slice(c * L, (c + 1) * L), slice(p * LANES, (p + 1) * LANES))
             for c in range(rows // L) for p in range(n_pairs)]
    qs = range(len(probs))
    g = [_dot_nt(jnp.concatenate([at[q], rt[q]], axis=0),
                 jnp.concatenate([stack(bt[q]), stack(kt[q])], axis=0)) for q in probs]
    a_ab = [jnp.where(strict, x[:L, :LANES], 0.0) for x in g]
    a_ak = [jnp.where(strict, x[:L, LANES:], 0.0).astype(BF16) for x in g]
    a_rb = [jnp.where(incl, x[L:, :LANES], 0.0).astype(BF16) for x in g]
    a_rk = [jnp.where(incl, x[L:, LANES:], 0.0).astype(BF16) for x in g]
    v2 = [stack(vb[q]) for q in probs]
    avy = [_dot(jnp.concatenate([a_ak[q], a_rk[q]], axis=0), v2[q]) for q in qs]
    minv = [eye + x for x in a_ab]
    apow = [x.astype(BF16) for x in a_ab]
    apow = [_dot(x, stack(x)).astype(BF16) for x in apow]
    for i in range(4):
        res = [_dot(apow[q], jnp.concatenate([stack(minv[q].astype(BF16)), stack(apow[q])], axis=1))
               for q in qs]
        minv = [minv[q] + res[q][:, :LANES] for q in qs]
        apow = [x[:, LANES:].astype(BF16) for x in res]
    minv = [(minv[q] + _dot(apow[q], stack(minv[q].astype(BF16)))).astype(BF16) for q in qs]
    mw = [_dot(minv[q], jnp.concatenate([stack(at[probs[q]]), stack(avy[q][:L].astype(BF16))], axis=1))
          for q in qs]

    n_states = n_seq * n_pairs
    s = [s_scr[i] for i in range(n_states)]
    for c in range(n_chunks):
        cq = [(b * n_chunks + c) * n_pairs + p for b in range(n_seq) for p in range(n_pairs)]
        sb = [x.astype(BF16) for x in s]
        us = [_dot_nt(jnp.concatenate([mw[q][:, :LANES].astype(BF16), rt[probs[q]]], axis=0), sb[i])
              for i, q in enumerate(cq)]
        u2 = [stack((us[i][:L] + mw[q][:, LANES:]).astype(BF16)) for i, q in enumerate(cq)]
        for i, q in enumerate(cq):
            y_scr[probs[q]] = us[i][L:] + _dot(a_rb[q], u2[i]) + avy[q][L:]
        upd = [_dot_tn(jnp.concatenate([u2[i], v2[q]], axis=0),
                       jnp.concatenate([stack(bt[probs[q]]), stack(kt[probs[q]])], axis=0))
               for i, q in enumerate(cq)]
        s = [(s[i] + upd[i]) * p_last[q // n_pairs][:, probs[q][1]] for i, q in enumerate(cq)]
    for i in range(n_states):
        s_scr[i] = s[i]

    y = y_scr[...]
    yc = y - _head_sum(y, hsum_ref[...]) * (1.0 / HEAD_DIM)
    var = _head_sum(yc * yc, hsum_ref[...]) * (1.0 / HEAD_DIM)
    yn = yc * lax.rsqrt(var + GN_EPS) * lnx_ref[0:1, :] + lnx_ref[1:2, :]
    out = _dot(((yn + flat(bonus_ref)) * flat(gate_ref)).astype(BF16), wout_ref[...])
    o_ref[...] = (flat(h_ref) + _rms(out, g_ref[3:4, :])).reshape(n_seq, seq_rows, d)


def _wkv(rt, at, kt, bt, v, plast, h, bonus, gate, g, lnx, hsum, w_out, *, seq, n_seq=2, n_chunks=2):
    t, d = rt.shape
    bsz = t // seq
    n_seq = min(n_seq, bsz)
    seq_rows = n_chunks * CHUNK
    tile = pl.BlockSpec((n_seq, seq_rows, d), lambda i, j: (i, j, 0))
    by_seq = lambda x: x.reshape(bsz, seq, d)
    out = pl.pallas_call(
        _wkv_kernel,
        grid=(bsz // n_seq, seq // seq_rows),
        in_specs=[tile] * 5 + [pl.BlockSpec((n_seq, n_chunks, 1, d), lambda i, j: (i, j, 0, 0))] + [tile] * 3
        + [_resident(g.shape), _resident(lnx.shape), _resident(hsum.shape),
                               _resident(w_out.shape)],
        out_specs=tile,
        out_shape=jax.ShapeDtypeStruct((bsz, seq, d), F32),
        scratch_shapes=[pltpu.VMEM((n_seq * (d // LANES), LANES, LANES), F32),
                        pltpu.VMEM((n_seq * seq_rows, d), F32)],
        compiler_params=_params(2),
        name="wkv",
    )(*(by_seq(x) for x in (rt, at, kt, bt, v)), plast.reshape(bsz, seq // CHUNK, 1, d),
      *(by_seq(x) for x in (h, bonus, gate)), g, lnx, hsum, w_out)
    return out.reshape(t, d)


def kernel(x, p, norm_g, ffn_w13, ffn_w2, ple_w_gate, ple_w_proj, a_w_in, a_b_in, a_v_gain, a_w_s, a_b_s, a_w_out, b_mu, b_w_in, b_w0, b_w1, b_w2, b_a0, b_a1, b_a2, b_g1, b_g2, b_k_k, b_k_a, b_r_k, b_lnx, b_w_out, b_v0, b_v1, b_v2):
    bsz, seq, d = x.shape
    depth = norm_g.shape[0]
    t = bsz * seq
    bf = lambda w: w.astype(BF16)
    row = lambda w: w.reshape(1, -1)

    head = jnp.arange(MXU_WIDTH) // HEAD_DIM
    hsum = (head[:, None] == head[None, :]).astype(BF16)

    h = x.reshape(t, d)
    p_flat = p.reshape(depth, t, -1)
    v_first = None
    for i in range(depth):
        g = norm_g[i]
        j = i // 2
        h = _ffn(h, g, ffn_w13, ffn_w2, i, 0, g_in=0, g_out=1)
        if i % 2 == 0:
            h = _gmlp(h, g, a_w_in, row(a_b_in[j]), row(a_v_gain[j]), a_w_s[j], a_b_s[j].T, a_w_out, j)
        else:
            vres = None
            if v_first is not None:
                vres = (v_first, row(b_v0[j - 1]), bf(b_v1[j - 1]), bf(b_v2[j - 1]))
            rt, at, kt, bt, v, plast, gate, bonus = _rwkv_in(
                h, g, b_mu[j], bf(b_w_in[j, 0]), bf(b_w_in[j, 1]), bf(b_w_in[j, 2]),
                row(b_w0[j]), bf(b_w1[j]), bf(b_w2[j]), row(b_a0[j]), bf(b_a1[j]), bf(b_a2[j]),
                bf(b_g1[j]), bf(b_g2[j]), row(b_k_k[j]), row(b_k_a[j]), row(b_r_k[j]), hsum,
                vres, seq=seq)
            if v_first is None:
                v_first = v
            h = _wkv(rt, at, kt, bt, v, plast, h, bonus, gate, g, b_lnx[j], hsum, bf(b_w_out[j]), seq=seq)
        h = _ffn(h, g, ffn_w13, ffn_w2, i, 1, g_in=4, g_out=5,
                 ple=(p_flat, bf(ple_w_gate[i]), bf(ple_w_proj[i])))
    return h.reshape(bsz, seq, d)
```
